```python
import math
import jax, jax.numpy as jnp
from jax import lax
import numpy as np

D_MODEL = 2048
BATCH = 4
SEQ = 4096
DEPTH = 4

GRID_W = 64
CTX_LEN = 256
N_BRANCH = 4
BRANCH_W = D_MODEL // 4
V_HEAD = 64
MLA_HEADS = BRANCH_W // V_HEAD
QK_NOPE = 64
QK_ROPE = 32
QK_HEAD = QK_NOPE + QK_ROPE
ROPE_PAIRS = QK_ROPE // 4
Q_LORA = D_MODEL // 4
KV_LORA = D_MODEL // 8
ROPE_THETA = 10000.0
Q_BLOCK = 128
CONF_W = BRANCH_W
CONF_K = 31
SC_W = BRANCH_W
SC_K = 3
FNET_W = BRANCH_W
FNET_GROUPS = 4
FNET_GW = FNET_W // FNET_GROUPS
N_EXPERTS = 16
EXPERT_FF = D_MODEL // 2
EC_CAPACITY_FACTOR = 2
EPS = 1e-6

OFF_CQ = 0
OFF_CKV = OFF_CQ + Q_LORA
OFF_KR = OFF_CKV + KV_LORA
OFF_CONF = OFF_KR + QK_ROPE
OFF_SC = OFF_CONF + 2 * CONF_W
OFF_FNET = OFF_SC + 3 * SC_W
IN_W = OFF_FNET + FNET_W

kernel_name = "hybrid_mla_conv_fnet_ec_moe_dit"


def rmsnorm(x, g):
    x32 = x.astype(jnp.float32)
    y = x32 * lax.rsqrt(jnp.mean(x32 * x32, axis=-1, keepdims=True) + EPS)
    return (y * g.astype(jnp.float32)).astype(x.dtype)


def layernorm(x, g, b):
    x32 = x.astype(jnp.float32)
    mu = jnp.mean(x32, axis=-1, keepdims=True)
    var = jnp.mean(jnp.square(x32 - mu), axis=-1, keepdims=True)
    y = (x32 - mu) * lax.rsqrt(var + EPS)
    return (y * g.astype(jnp.float32) + b.astype(jnp.float32)).astype(x.dtype)


def axial_rope_tables(n_tokens):
    rows = n_tokens // GRID_W
    t_row = jnp.repeat(jnp.arange(rows, dtype=jnp.float32), GRID_W)
    t_col = jnp.tile(jnp.arange(GRID_W, dtype=jnp.float32), rows)
    inv = ROPE_THETA ** (-jnp.arange(ROPE_PAIRS, dtype=jnp.float32) / ROPE_PAIRS)
    ang = jnp.stack([t_row[:, None] * inv, t_col[:, None] * inv], axis=1)
    return jnp.cos(ang), jnp.sin(ang)


def apply_axial_rope(x, cos, sin):
    xr = x.reshape(x.shape[:-1] + (2, 2, ROPE_PAIRS))
    x1, x2 = xr[..., 0, :], xr[..., 1, :]
    c = cos[None, :, None].astype(x.dtype)
    s = sin[None, :, None].astype(x.dtype)
    out = jnp.stack([x1 * c - x2 * s, x2 * c + x1 * s], axis=-2)
    return out.reshape(x.shape)


def mla_queries(u, g_q_lora, w_uq, g_qn, rope):
    b, n, _ = u.shape
    q = (rmsnorm(u[..., OFF_CQ:OFF_CKV], g_q_lora) @ w_uq).reshape(b, n, MLA_HEADS, QK_HEAD)
    q = rmsnorm(q, g_qn)
    if rope is not None:
        q = jnp.concatenate([q[..., :QK_NOPE], apply_axial_rope(q[..., QK_NOPE:], *rope)], axis=-1)
    return q


def mla_keys_values(u, g_kv_lora, w_uk, w_uv, g_kn, rope):
    b, n, _ = u.shape
    c_kv = rmsnorm(u[..., OFF_CKV:OFF_KR], g_kv_lora)
    k_nope = (c_kv @ w_uk).reshape(b, n, MLA_HEADS, QK_NOPE)
    v = (c_kv @ w_uv).reshape(b, n, MLA_HEADS, V_HEAD)
    k_rope = jnp.broadcast_to(u[..., None, OFF_KR:OFF_CONF], (b, n, MLA_HEADS, QK_ROPE))
    k = rmsnorm(jnp.concatenate([k_nope, k_rope], axis=-1), g_kn)
    if rope is not None:
        k = jnp.concatenate([k[..., :QK_NOPE], apply_axial_rope(k[..., QK_NOPE:], *rope)], axis=-1)
    return k, v


def attend_dense(q, k, v):
    s = jnp.einsum('bqhd,bkhd->bhqk', q, k).astype(jnp.float32) * (QK_HEAD ** -0.5)
    p = jax.nn.softmax(s, axis=-1).astype(v.dtype)
    return jnp.einsum('bhqk,bkhd->bqhd', p, v)


def attend_blocked(q, k, v):
    b, n, h, d = q.shape
    qb = jnp.moveaxis(q.reshape(b, n // Q_BLOCK, Q_BLOCK, h, d), 1, 0)
    ob = lax.map(lambda qi: attend_dense(qi, k, v), qb)
    return jnp.moveaxis(ob, 0, 1).reshape(b, n, h * V_HEAD)


def depthwise_conv(x, w):
    return lax.conv_general_dilated(x, w[:, None, :].astype(x.dtype), window_strides=(1,), padding='SAME',
                                    dimension_numbers=('NWC', 'WIO', 'NWC'), feature_group_count=x.shape[-1])


def local_branches(u, conf_dw, conf_ln_g, conf_ln_b, sc_dw):
    b, n, _ = u.shape
    a = u[..., OFF_CONF:OFF_CONF + CONF_W]
    gt = u[..., OFF_CONF + CONF_W:OFF_SC]
    y = depthwise_conv(a * jax.nn.sigmoid(gt), conf_dw)
    y = jax.nn.silu(layernorm(y, conf_ln_g, conf_ln_b))
    bg, cg, xv = jnp.split(u[..., OFF_SC:OFF_FNET], 3, axis=-1)
    z = bg * depthwise_conv(cg * xv, sc_dw)
    f = u[..., OFF_FNET:IN_W].reshape(b, n, FNET_GROUPS, FNET_GW).astype(jnp.float32)
    f = jnp.fft.fft2(f, axes=(1, 3), norm='ortho').real.astype(u.dtype).reshape(b, n, FNET_W)
    return jnp.concatenate([y, z, f], axis=-1)


def merge_branches(h, y, w_branch, w_gate, w_out):
    b, n, _ = h.shape
    br = jnp.einsum('bnkc,kcd->bnkd', y.reshape(b, n, N_BRANCH, BRANCH_W), w_branch)
    g = jax.nn.sigmoid((h @ w_gate).reshape(b, n, N_BRANCH, D_MODEL))
    return jnp.einsum('bnkd,bnkd->bnd', g, br) @ w_out


def expert_choice_ffn(h, w_router, w_e1, w_e3, w_e2):
    b, n, _ = h.shape
    cap = EC_CAPACITY_FACTOR * n // N_EXPERTS
    aff = jax.nn.softmax((h @ w_router).astype(jnp.float32), axis=-1)
    gate, idx = lax.top_k(jnp.swapaxes(aff, 1, 2), cap)
    xg = jax.vmap(lambda hb, ib: hb[ib])(h, idx)
    hid = jax.nn.silu(jnp.einsum('becd,edf->becf', xg, w_e1)) * jnp.einsum('becd,edf->becf', xg, w_e3)
    ye = jnp.einsum('becf,efd->becd', hid, w_e2) * gate[..., None].astype(h.dtype)
    bidx = jnp.arange(b)[:, None, None]
    return jnp.zeros_like(h).at[bidx, idx].add(ye)


def setup_inputs(seed: int = 0) -> dict:
    key = jax.random.key(seed)
    ks = jax.random.split(key, 32)
    f32 = jnp.float32
    L, D, E, F = DEPTH, D_MODEL, N_EXPERTS, EXPERT_FF

    def nrm(k, shape, scale):
        return jax.random.normal(k, shape, f32) * scale

    def gain(k, shape):
        return 1.0 + 0.1 * jax.random.normal(k, shape, f32)

    return {
        "x": nrm(ks[0], (BATCH, SEQ, D), 1.0),
        "c": nrm(ks[1], (BATCH, D), 1.0),
        "ctx": nrm(ks[2], (BATCH, CTX_LEN, D), 1.0),
        "c_ctx": nrm(ks[3], (D,), 1.0),
        "w_mod": nrm(ks[4], (L, D, 6 * D), 0.5 * D ** -0.5),
        "b_mod": nrm(ks[5], (L, 6 * D), 0.02),
        "g_mix": gain(ks[6], (L, D)),
        "g_ffn": gain(ks[7], (L, D)),
        "w_in": nrm(ks[8], (L, D, IN_W), D ** -0.5),
        "g_q_lora": gain(ks[9], (L, Q_LORA)),
        "w_uq": nrm(ks[10], (L, Q_LORA, MLA_HEADS * QK_HEAD), Q_LORA ** -0.5),
        "g_kv_lora": gain(ks[11], (L, KV_LORA)),
        "w_uk": nrm(ks[12], (L, KV_LORA, MLA_HEADS * QK_NOPE), KV_LORA ** -0.5),
        "w_uv": nrm(ks[13], (L, KV_LORA, MLA_HEADS * V_HEAD), KV_LORA ** -0.5),
        "g_qn": gain(ks[14], (L, QK_HEAD)),
        "g_kn": gain(ks[15], (L, QK_HEAD)),
        "conf_dw": nrm(ks[16], (L, CONF_K, CONF_W), CONF_K ** -0.5),
        "conf_ln_g": gain(ks[17], (L, CONF_W)),
        "conf_ln_b": nrm(ks[18], (L, CONF_W), 0.02),
        "sc_dw": nrm(ks[19], (L, SC_K, SC_W), SC_K ** -0.5),
        "w_branch": nrm(ks[20], (L, N_BRANCH, BRANCH_W, D), BRANCH_W ** -0.5),
        "w_gate": nrm(ks[21], (L, D, N_BRANCH * D), D ** -0.5),
        "w_out": nrm(ks[22], (L, D, D), D ** -0.5),
        "w_router": nrm(ks[23], (L, D, E), D ** -0.5),
        "w_e1": nrm(ks[24], (L, E, D, F), D ** -0.5),
        "w_e3": nrm(ks[25], (L, E, D, F), D ** -0.5),
        "w_e2": nrm(ks[26], (L, E, F, D), F ** -0.5),
    }


def reference(x, c, ctx, c_ctx, w_mod, b_mod, g_mix, g_ffn, w_in, g_q_lora, w_uq, g_kv_lora, w_uk, w_uv,
              g_qn, g_kn, conf_dw, conf_ln_g, conf_ln_b, sc_dw, w_branch, w_gate, w_out, w_router,
              w_e1, w_e3, w_e2):
    rope = axial_rope_tables(x.shape[1])
    s_lat = jax.nn.silu(c)
    s_ctx = jax.nn.silu(c_ctx)
    xc = ctx
    for l in range(DEPTH):
        last = l == DEPTH - 1
        sh1, s1, g1, sh2, s2, g2 = [m[:, None, :] for m in jnp.split(s_lat @ w_mod[l] + b_mod[l], 6, axis=-1)]
        sh1c, s1c, g1c, sh2c, s2c, g2c = jnp.split(s_ctx @ w_mod[l] + b_mod[l], 6, axis=-1)

        hx = rmsnorm(x, g_mix[l]) * (1.0 + s1) + sh1
        hc = rmsnorm(xc, g_mix[l]) * (1.0 + s1c) + sh1c
        u_x = hx @ w_in[l]
        u_c = hc @ (w_in[l][:, :OFF_CONF] if last else w_in[l])

        k_c, v_c = mla_keys_values(u_c, g_kv_lora[l], w_uk[l], w_uv[l], g_kn[l], None)
        q_x = mla_queries(u_x, g_q_lora[l], w_uq[l], g_qn[l], rope)
        k_x, v_x = mla_keys_values(u_x, g_kv_lora[l], w_uk[l], w_uv[l], g_kn[l], rope)
        att_x = attend_blocked(q_x, jnp.concatenate([k_c, k_x], axis=1), jnp.concatenate([v_c, v_x], axis=1))
        y_x = jnp.concatenate([att_x, local_branches(u_x, conf_dw[l], conf_ln_g[l], conf_ln_b[l], sc_dw[l])], axis=-1)
        x = x + g1 * merge_branches(hx, y_x, w_branch[l], w_gate[l], w_out[l])

        if not last:
            q_c = mla_queries(u_c, g_q_lora[l], w_uq[l], g_qn[l], None)
            b_c, n_c = q_c.shape[0], q_c.shape[1]
            att_c = attend_dense(q_c, k_c, v_c).reshape(b_c, n_c, MLA_HEADS * V_HEAD)
            y_c = jnp.concatenate([att_c, local_branches(u_c, conf_dw[l], conf_ln_g[l], conf_ln_b[l], sc_dw[l])], axis=-1)
            xc = xc + g1c * merge_branches(hc, y_c, w_branch[l], w_gate[l], w_out[l])

        hx2 = rmsnorm(x, g_ffn[l]) * (1.0 + s2) + sh2
        x = x + g2 * expert_choice_ffn(hx2, w_router[l], w_e1[l], w_e3[l], w_e2[l])
        if not last:
            hc2 = rmsnorm(xc, g_ffn[l]) * (1.0 + s2c) + sh2c
            xc = xc + g2c * expert_choice_ffn(hc2, w_router[l], w_e1[l], w_e3[l], w_e2[l])
    return x
```

```python
import functools
import math

import jax
import jax.numpy as jnp
from jax import lax
from jax.experimental import pallas as pl
from jax.experimental.pallas import tpu as pltpu

F32 = jnp.float32
BF16 = jnp.bfloat16

D_MODEL = 2048
DEPTH = 4
GRID_W = 64
N_BRANCH = 4
BRANCH_W = 512
MLA_HEADS = 8
V_HEAD = 64
QK_NOPE = 64
QK_ROPE = 32
QK_HEAD = 96
ROPE_PAIRS = 8
Q_LORA = 512
KV_LORA = 256
ROPE_THETA = 10000.0
CONF_K = 31
SC_K = 3
FNET_GROUPS = 4
FNET_GW = 128
N_EXPERTS = 16
EXPERT_FF = 1024
EC_CAPACITY_FACTOR = 2
EPS = 1e-6
OFF_KR = 768
OFF_CONF = 800

LANES = 128
HALO = 16
IN_PAD_W = 4096
COL_BLK = 512
VMEM_LIMIT = 56 * 1024 * 1024

_NT = (((1,), (1,)), ((), ()))


def _cparams(sem):
    return pltpu.CompilerParams(dimension_semantics=sem, vmem_limit_bytes=VMEM_LIMIT)


def _dot(a, b):
    return jnp.dot(a, b, preferred_element_type=F32)


def _dot_nt(a, b):
    return lax.dot_general(a, b, _NT, preferred_element_type=F32)


def _split2(a):
    hi = a.astype(BF16)
    lo = (a - hi.astype(F32)).astype(BF16)
    return hi, lo


def _dot3(a, b, nt=False):
    f = _dot_nt if nt else _dot
    ah, al = _split2(a)
    bh, bl = _split2(b)
    return f(ah, bh) + (f(al, bh) + f(ah, bl))


def _rms(x, g, width):
    ms = jnp.sum(x * x, axis=-1, keepdims=True) * (1.0 / width)
    return x * lax.rsqrt(ms + EPS) * g


def _silu(x):
    return x * jax.nn.sigmoid(x)


def _mod_kernel(s_ref, w_ref, b_ref, o_ref):
    s = _silu(s_ref[...])
    o_ref[...] = _dot3(s, w_ref[...]) + b_ref[...]


def _mod_all(cond, w_mod, b_mod):
    nl = w_mod.shape[0]
    tn = 1024
    return pl.pallas_call(
        _mod_kernel,
        out_shape=jax.ShapeDtypeStruct((nl, 8, 6 * D_MODEL), F32),
        grid=(nl, 6 * D_MODEL // tn),
        in_specs=[
            pl.BlockSpec((8, D_MODEL), lambda l, j: (0, 0)),
            pl.BlockSpec((None, D_MODEL, tn), lambda l, j: (l, 0, j)),
            pl.BlockSpec((None, 1, tn), lambda l, j: (l, 0, j)),
        ],
        out_specs=pl.BlockSpec((None, 8, tn), lambda l, j: (l, 0, j)),
        compiler_params=_cparams(("parallel", "parallel")),
        name="adaln_mod",
    )(cond, w_mod, b_mod.reshape(nl, 1, 6 * D_MODEL))


def _in_proj_kernel(x_ref, g_ref, sh_ref, sc_ref, w_ref, u_ref, hx_ref):
    @pl.when(pl.program_id(1) == 0)
    def _():
        y = _rms(x_ref[...], g_ref[...], D_MODEL)
        hx_ref[...] = (y * (1.0 + sc_ref[0]) + sh_ref[0]).astype(BF16)

    u_ref[...] = _dot(hx_ref[...], w_ref[...])


def _in_proj(x2, g_all, mod3, w_all, l, n):
    m = x2.shape[0]
    tm = min(512, n)
    tpb = n // tm
    tn = 1024
    return pl.pallas_call(
        _in_proj_kernel,
        out_shape=(jax.ShapeDtypeStruct((m, IN_PAD_W), F32), jax.ShapeDtypeStruct((m, D_MODEL), BF16)),
        grid=(m // tm, IN_PAD_W // tn),
        in_specs=[
            pl.BlockSpec((tm, D_MODEL), lambda i, j: (i, 0)),
            pl.BlockSpec((None, 1, D_MODEL), lambda i, j: (l, 0, 0)),
            pl.BlockSpec((1, 1, D_MODEL), lambda i, j: (i // tpb, 0, 0)),
            pl.BlockSpec((1, 1, D_MODEL), lambda i, j: (i // tpb, 0, 1)),
            pl.BlockSpec((None, D_MODEL, tn), lambda i, j: (l, 0, j)),
        ],
        out_specs=(
            pl.BlockSpec((tm, tn), lambda i, j: (i, j)),
            pl.BlockSpec((tm, D_MODEL), lambda i, j: (i, 0)),
        ),
        compiler_params=_cparams(("parallel", "arbitrary")),
        name="norm_in_proj",
    )(x2, g_all, mod3, mod3, w_all)


def _head_norm_rope(xh, gn, c, s1, s2):
    xh = _rms(xh, gn, QK_HEAD)
    return xh * c + pltpu.roll(xh, LANES - ROPE_PAIRS, 1) * s1 + pltpu.roll(xh, ROPE_PAIRS, 1) * s2


def _q_kernel(cq_ref, gl_ref, w_ref, gn_ref, c_ref, s1_ref, s2_ref, q_ref):
    y = _rms(cq_ref[...], gl_ref[...], Q_LORA).astype(BF16)
    q = _dot(y, w_ref[...])
    gn = gn_ref[...]
    c, s1, s2 = c_ref[...], s1_ref[...], s2_ref[...]
    for h in range(MLA_HEADS):
        qh = _head_norm_rope(q[:, h * LANES:(h + 1) * LANES], gn, c, s1, s2)
        q_ref[0, h] = (qh * (QK_HEAD ** -0.5)).astype(BF16)


def _q_proj(u, gl_all, w_all, gn_all, rope, l, b, n):
    m = u.shape[0]
    tm = min(512, n)
    tpb = n // tm
    tab = pl.BlockSpec((tm, LANES), lambda i: (i % tpb, 0))
    return pl.pallas_call(
        _q_kernel,
        out_shape=jax.ShapeDtypeStruct((b, MLA_HEADS, n, LANES), BF16),
        grid=(m // tm,),
        in_specs=[
            pl.BlockSpec((tm, COL_BLK), lambda i: (i, 0)),
            pl.BlockSpec((None, 1, Q_LORA), lambda i: (l, 0, 0)),
            pl.BlockSpec((None, Q_LORA, MLA_HEADS * LANES), lambda i: (l, 0, 0)),
            pl.BlockSpec((None, 1, LANES), lambda i: (l, 0, 0)),
            tab, tab, tab,
        ],
        out_specs=pl.BlockSpec((1, MLA_HEADS, tm, LANES), lambda i: (i // tpb, 0, i % tpb, 0)),
        compiler_params=_cparams(("parallel",)),
        name="mla_q",
    )(u, gl_all, w_all, gn_all, *rope)


def _kv_kernel(ck_ref, gl_ref, wk_ref, wv_ref, gn_ref, c_ref, s1_ref, s2_ref, k_ref, v_ref):
    blk = ck_ref[...]
    y = _rms(blk[:, :KV_LORA], gl_ref[...], KV_LORA).astype(BF16)
    kn = _dot(y, wk_ref[...])
    v = _dot(y, wv_ref[...])
    kr = blk[:, KV_LORA:KV_LORA + LANES]
    gn = gn_ref[...]
    c, s1, s2 = c_ref[...], s1_ref[...], s2_ref[...]
    for h in range(MLA_HEADS):
        kh = _head_norm_rope(kn[:, h * LANES:(h + 1) * LANES] + kr, gn, c, s1, s2)
        k_ref[0, h] = kh.astype(BF16)
        v_ref[0, h] = v[:, h * LANES:(h + 1) * LANES].astype(BF16)


def _kv_proj(u, gl_all, wk_all, wv_all, gn_all, rope, l, b, n):
    m = u.shape[0]
    tm = min(512, n)
    tpb = n // tm
    tab = pl.BlockSpec((tm, LANES), lambda i: (i % tpb, 0))
    hw = MLA_HEADS * LANES
    o_spec = pl.BlockSpec((1, MLA_HEADS, tm, LANES), lambda i: (i // tpb, 0, i % tpb, 0))
    o_shape = jax.ShapeDtypeStruct((b, MLA_HEADS, n, LANES), BF16)
    return pl.pallas_call(
        _kv_kernel,
        out_shape=(o_shape, o_shape),
        grid=(m // tm,),
        in_specs=[
            pl.BlockSpec((tm, COL_BLK), lambda i: (i, 1)),
            pl.BlockSpec((None, 1, KV_LORA), lambda i: (l, 0, 0)),
            pl.BlockSpec((None, KV_LORA, hw), lambda i: (l, 0, 0)),
            pl.BlockSpec((None, KV_LORA, hw), lambda i: (l, 0, 0)),
            pl.BlockSpec((None, 1, LANES), lambda i: (l, 0, 0)),
            tab, tab, tab,
        ],
        out_specs=(o_spec, o_spec),
        compiler_params=_cparams(("parallel",)),
        name="mla_kv",
    )(u, gl_all, wk_all, wv_all, gn_all, *rope)


def _attn_kernel(*refs, n_src, key_chunk):
    q_ref = refs[0]
    kv = refs[1:1 + 2 * n_src]
    o_ref = refs[1 + 2 * n_src]
    s_scr = refs[2 + 2 * n_src]
    chunks = []
    col = 0
    for si in range(n_src):
        nk = kv[2 * si].shape[2]
        for st in range(0, nk, key_chunk):
            sz = min(key_chunk, nk - st)
            chunks.append((si, st, sz, col))
            col += sz
    out = None
    for hh in range(2):
        q = q_ref[0, hh]
        mx = None
        for si, st, sz, co in chunks:
            s = _dot_nt(q, kv[2 * si][0, hh, st:st + sz, :])
            s_scr[:, co:co + sz] = s
            cm = jnp.max(s, axis=-1, keepdims=True)
            mx = cm if mx is None else jnp.maximum(mx, cm)
        den = None
        acc = None
        for si, st, sz, co in chunks:
            p = jnp.exp(s_scr[:, co:co + sz] - mx)
            ps = jnp.sum(p, axis=-1, keepdims=True)
            den = ps if den is None else den + ps
            pv = _dot(p.astype(BF16), kv[2 * si + 1][0, hh, st:st + sz, :])
            acc = pv if acc is None else acc + pv
        o = acc / den
        out = o if out is None else out + o
    o_ref[0] = out.astype(BF16)


def _attention(q, kvs):
    b, _, nq, _ = q.shape
    tq = min(256, nq)
    n_src = len(kvs) // 2
    nk_tot = sum(kvs[2 * i].shape[2] for i in range(n_src))
    in_specs = [pl.BlockSpec((1, 2, tq, LANES), lambda bb, hp, i: (bb, hp, i, 0))]
    for a in kvs:
        in_specs.append(pl.BlockSpec((1, 2, a.shape[2], LANES), lambda bb, hp, i: (bb, hp, 0, 0)))
    return pl.pallas_call(
        functools.partial(_attn_kernel, n_src=n_src, key_chunk=1024),
        out_shape=jax.ShapeDtypeStruct((b, nq, BRANCH_W), BF16),
        grid=(b, MLA_HEADS // 2, nq // tq),
        in_specs=in_specs,
        out_specs=pl.BlockSpec((1, tq, LANES), lambda bb, hp, i: (bb, i, hp)),
        scratch_shapes=[pltpu.VMEM((tq, nk_tot), F32)],
        compiler_params=_cparams(("parallel", "parallel", "parallel")),
        name="mla_attention",
    )(q, *kvs)


def _local_kernel(a_ref, g_ref, sb_ref, scc_ref, sx_ref,
                  ap_ref, gp_ref, cp_ref, xp_ref, an_ref, gn_ref, cn_ref, xn_ref,
                  cw_ref, lg_ref, lb_ref, sw_ref, y_ref, z_ref, g_scr, h_scr, *, tpb, rows):
    i = pl.program_id(0)
    tm = a_ref.shape[0]
    has_prev = (i % tpb) > 0
    has_next = (i % tpb) < tpb - 1
    zero = jnp.zeros((HALO, COL_BLK), F32)
    g_scr[0:HALO, :] = jnp.where(has_prev, ap_ref[...] * jax.nn.sigmoid(gp_ref[...]), zero)
    g_scr[HALO:HALO + tm, :] = a_ref[...] * jax.nn.sigmoid(g_ref[...])
    g_scr[HALO + tm:, :] = jnp.where(has_next, an_ref[...] * jax.nn.sigmoid(gn_ref[...]), zero)
    h_scr[0:HALO, :] = jnp.where(has_prev, cp_ref[...] * xp_ref[...], zero)
    h_scr[HALO:HALO + tm, :] = scc_ref[...] * sx_ref[...]
    h_scr[HALO + tm:, :] = jnp.where(has_next, cn_ref[...] * xn_ref[...], zero)
    lg, lb = lg_ref[...], lb_ref[...]
    for c in range(tm // rows):
        r0 = c * rows
        acc = None
        for j in range(CONF_K):
            st = r0 + HALO - CONF_K // 2 + j
            t = cw_ref[j:j + 1, :] * g_scr[st:st + rows, :]
            acc = t if acc is None else acc + t
        mu = jnp.mean(acc, axis=-1, keepdims=True)
        dv = acc - mu
        var = jnp.mean(dv * dv, axis=-1, keepdims=True)
        y = dv * lax.rsqrt(var + EPS) * lg + lb
        y_ref[r0:r0 + rows, :] = _silu(y).astype(BF16)
        acc = None
        for j in range(SC_K):
            st = r0 + HALO - SC_K // 2 + j
            t = sw_ref[j:j + 1, :] * h_scr[st:st + rows, :]
            acc = t if acc is None else acc + t
        z_ref[r0:r0 + rows, :] = (sb_ref[r0:r0 + rows, :] * acc).astype(BF16)


def _local_branches(u, cw_all, lg_all, lb_all, sw_all, l, n):
    m = u.shape[0]
    tm = min(512, n)
    tpb = n // tm
    hb = tm // HALO
    last = m // HALO - 1

    def cur(cb):
        return pl.BlockSpec((tm, COL_BLK), lambda i: (i, cb))

    def prev(cb):
        return pl.BlockSpec((HALO, COL_BLK), lambda i: (jnp.maximum(i * hb - 1, 0), cb))

    def nxt(cb):
        return pl.BlockSpec((HALO, COL_BLK), lambda i: (jnp.minimum((i + 1) * hb, last), cb))

    o_shape = jax.ShapeDtypeStruct((m, COL_BLK), BF16)
    o_spec = pl.BlockSpec((tm, COL_BLK), lambda i: (i, 0))
    return pl.pallas_call(
        functools.partial(_local_kernel, tpb=tpb, rows=32),
        out_shape=(o_shape, o_shape),
        grid=(m // tm,),
        in_specs=[
            cur(2), cur(3), cur(4), cur(5), cur(6),
            prev(2), prev(3), prev(5), prev(6),
            nxt(2), nxt(3), nxt(5), nxt(6),
            pl.BlockSpec((None, CONF_K, COL_BLK), lambda i: (l, 0, 0)),
            pl.BlockSpec((None, 1, COL_BLK), lambda i: (l, 0, 0)),
            pl.BlockSpec((None, 1, COL_BLK), lambda i: (l, 0, 0)),
            pl.BlockSpec((None, SC_K, COL_BLK), lambda i: (l, 0, 0)),
        ],
        out_specs=(o_spec, o_spec),
        scratch_shapes=[pltpu.VMEM((tm + 2 * HALO, COL_BLK), F32), pltpu.VMEM((tm + 2 * HALO, COL_BLK), F32)],
        compiler_params=_cparams(("parallel",)),
        name="local_convs",
    )(u, u, u, u, u, u, u, u, u, u, u, u, u, cw_all, lg_all, lb_all, sw_all)


def _chan_dft_kernel(f_ref, w_ref, o_ref):
    r = _dot(f_ref[...].astype(BF16), w_ref[...])
    o_ref[0, 0] = r[:, :COL_BLK].astype(BF16)
    o_ref[0, 1] = r[:, COL_BLK:].astype(BF16)


def _chan_dft(u, w_chan, b, n):
    m = u.shape[0]
    tm = min(512, n)
    tpb = n // tm
    return pl.pallas_call(
        _chan_dft_kernel,
        out_shape=jax.ShapeDtypeStruct((b, 2, n, COL_BLK), BF16),
        grid=(m // tm,),
        in_specs=[
            pl.BlockSpec((tm, COL_BLK), lambda i: (i, 7)),
            pl.BlockSpec((COL_BLK, 2 * COL_BLK), lambda i: (0, 0)),
        ],
        out_specs=pl.BlockSpec((1, 2, tm, COL_BLK), lambda i: (i // tpb, 0, i % tpb, 0)),
        compiler_params=_cparams(("parallel",)),
        name="fnet_chan_dft",
    )(u, w_chan)


def _pos_dft_kernel(t_ref, ab_ref, o_ref, acc_ref):
    k = pl.program_id(2)

    @pl.when(k == 0)
    def _():
        acc_ref[...] = jnp.zeros_like(acc_ref)

    acc_ref[...] += _dot(t_ref[...], ab_ref[0])

    @pl.when(k == pl.num_programs(2) - 1)
    def _():
        o_ref[0] = acc_ref[...].astype(BF16)


def _pos_dft(t_pos, ab):
    b, n2, _ = ab.shape
    n = n2 // 2
    tm = min(1024, n)
    tk = min(2048, n2)
    return pl.pallas_call(
        _pos_dft_kernel,
        out_shape=jax.ShapeDtypeStruct((b, n, COL_BLK), BF16),
        grid=(b, n // tm, n2 // tk),
        in_specs=[
            pl.BlockSpec((tm, tk), lambda bb, i, k: (i, k)),
            pl.BlockSpec((1, tk, COL_BLK), lambda bb, i, k: (bb, k, 0)),
        ],
        out_specs=pl.BlockSpec((1, tm, COL_BLK), lambda bb, i, k: (bb, i, 0)),
        scratch_shapes=[pltpu.VMEM((tm, COL_BLK), F32)],
        compiler_params=_cparams(("parallel", "parallel", "arbitrary")),
        name="fnet_pos_dft",
    )(t_pos, ab)


def _merge_kernel(hx_ref, y0_ref, y1_ref, y2_ref, y3_ref, wg0_ref, wg1_ref, wg2_ref, wg3_ref, wb_ref, o_ref):
    hx = hx_ref[...]
    ys = (y0_ref, y1_ref, y2_ref, y3_ref)
    wgs = (wg0_ref, wg1_ref, wg2_ref, wg3_ref)
    acc = None
    for k in range(N_BRANCH):
        t = jax.nn.sigmoid(_dot(hx, wgs[k][...])) * _dot(ys[k][...], wb_ref[k])
        acc = t if acc is None else acc + t
    o_ref[...] = acc.astype(BF16)


def _merge(hx, ys, wg_all, wb_all, l):
    m = hx.shape[0]
    tm = min(1024, m)
    tn = 512
    nj = D_MODEL // tn
    y_spec = pl.BlockSpec((tm, BRANCH_W), lambda i, j: (i, 0))
    wg_specs = [pl.BlockSpec((None, D_MODEL, tn), lambda i, j, k=k: (l, 0, k * nj + j)) for k in range(N_BRANCH)]
    return pl.pallas_call(
        _merge_kernel,
        out_shape=jax.ShapeDtypeStruct((m, D_MODEL), BF16),
        grid=(m // tm, nj),
        in_specs=[pl.BlockSpec((tm, D_MODEL), lambda i, j: (i, 0)), y_spec, y_spec, y_spec, y_spec,
                  *wg_specs,
                  pl.BlockSpec((None, N_BRANCH, BRANCH_W, tn), lambda i, j: (l, 0, 0, j))],
        out_specs=pl.BlockSpec((tm, tn), lambda i, j: (i, j)),
        compiler_params=_cparams(("parallel", "parallel")),
        name="branch_merge",
    )(hx, *ys, wg_all, wg_all, wg_all, wg_all, wb_all)


def _out_proj_kernel(m_ref, w_ref, x_ref, g_ref, o_ref):
    o_ref[...] = x_ref[...] + g_ref[0] * _dot(m_ref[...], w_ref[...])


def _out_proj(mm, w_all, x2, mod3, l, n):
    m = mm.shape[0]
    tm = min(1024, n)
    tpb = n // tm
    tn = 512
    nj = D_MODEL // tn
    return pl.pallas_call(
        _out_proj_kernel,
        out_shape=jax.ShapeDtypeStruct((m, D_MODEL), F32),
        grid=(m // tm, nj),
        in_specs=[
            pl.BlockSpec((tm, D_MODEL), lambda i, j: (i, 0)),
            pl.BlockSpec((None, D_MODEL, tn), lambda i, j: (l, 0, j)),
            pl.BlockSpec((tm, tn), lambda i, j: (i, j)),
            pl.BlockSpec((1, 1, tn), lambda i, j: (i // tpb, 0, 2 * nj + j)),
        ],
        out_specs=pl.BlockSpec((tm, tn), lambda i, j: (i, j)),
        compiler_params=_cparams(("parallel", "parallel")),
        name="out_proj_residual",
    )(mm, w_all, x2, mod3)


def _router_kernel(x_ref, g_ref, sh_ref, sc_ref, wr_ref, hx_ref, aff_ref):
    y = _rms(x_ref[...], g_ref[...], D_MODEL)
    hx = y * (1.0 + sc_ref[0]) + sh_ref[0]
    hx_ref[...] = hx
    lt = _dot3(wr_ref[...], hx, nt=True)
    e = jnp.exp(lt - jnp.max(lt, axis=0, keepdims=True))
    aff_ref[0] = e / jnp.sum(e, axis=0, keepdims=True)


def _router(x2, g_all, mod3, wr_t_all, l, b, n):
    m = x2.shape[0]
    tm = min(512, n)
    tpb = n // tm
    return pl.pallas_call(
        _router_kernel,
        out_shape=(jax.ShapeDtypeStruct((m, D_MODEL), F32), jax.ShapeDtypeStruct((b, N_EXPERTS, n), F32)),
        grid=(m // tm,),
        in_specs=[
            pl.BlockSpec((tm, D_MODEL), lambda i: (i, 0)),
            pl.BlockSpec((None, 1, D_MODEL), lambda i: (l, 0, 0)),
            pl.BlockSpec((1, 1, D_MODEL), lambda i: (i // tpb, 0, 3)),
            pl.BlockSpec((1, 1, D_MODEL), lambda i: (i // tpb, 0, 4)),
            pl.BlockSpec((None, N_EXPERTS, D_MODEL), lambda i: (l, 0, 0)),
        ],
        out_specs=(
            pl.BlockSpec((tm, D_MODEL), lambda i: (i, 0)),
            pl.BlockSpec((1, N_EXPERTS, tm), lambda i: (i // tpb, 0, i % tpb)),
        ),
        compiler_params=_cparams(("parallel",)),
        name="norm_router",
    )(x2, g_all, mod3, mod3, wr_t_all)


def _prefix_count(mask01, tri):
    n = mask01.shape[1]
    carry = jnp.zeros((mask01.shape[0], 1), F32)
    outs = []
    for c in range(n // LANES):
        loc = _dot(mask01[:, c * LANES:(c + 1) * LANES], tri) + carry
        outs.append(loc)
        carry = loc[:, LANES - 1:LANES]
    return jnp.concatenate(outs, axis=1)


def _topk_kernel(a_ref, idx_ref, gate_ref, cnt_ref, *, cap):
    a = a_ref[0]
    ne, n = a.shape
    bits = pltpu.bitcast(a, jnp.int32)

    def search(i, t):
        cand = t | (jnp.int32(1) << (30 - i))
        c = jnp.sum(jnp.where(bits >= cand, 1.0, 0.0), axis=-1, keepdims=True)
        return jnp.where(c >= cap, cand, t)

    thr = lax.fori_loop(0, 31, search, jnp.zeros((ne, 1), jnp.int32))
    gt = bits > thr
    eq = bits == thr
    need = cap - jnp.sum(jnp.where(gt, 1.0, 0.0), axis=-1, keepdims=True)
    row = lax.broadcasted_iota(jnp.int32, (LANES, LANES), 0)
    colm = lax.broadcasted_iota(jnp.int32, (LANES, LANES), 1)
    tri = jnp.where(row <= colm, 1.0, 0.0).astype(BF16)
    eq_rank = _prefix_count(jnp.where(eq, 1.0, 0.0).astype(BF16), tri)
    sel = gt | (eq & (eq_rank <= need))
    cnt = _prefix_count(jnp.where(sel, 1.0, 0.0).astype(BF16), tri)
    cnt_ref[0] = cnt.astype(jnp.int32)
    slot = jnp.where(sel, cnt - 1.0, -1.0)

    tok = lax.broadcasted_iota(jnp.int32, (1, n), 1)
    t_hi = (tok >> 6).astype(F32)
    t_lo = (tok & 63).astype(F32)
    r_iota = lax.broadcasted_iota(jnp.int32, (cap, 1), 0).astype(F32)
    zeros3 = jnp.zeros((3, n), F32)
    for e in range(ne):
        onehot = jnp.where(slot[e:e + 1, :] == r_iota, 1.0, 0.0).astype(BF16)
        ae = a[e:e + 1, :]
        a1 = ae.astype(BF16).astype(F32)
        a2 = (ae - a1).astype(BF16).astype(F32)
        a3 = ae - a1 - a2
        rows = jnp.concatenate([t_hi, t_lo, a1, a2, a3, zeros3], axis=0).astype(BF16)
        res = _dot_nt(rows, onehot)
        idx_ref[0, e:e + 1, :] = (res[0:1] * 64.0 + res[1:2]).astype(jnp.int32)
        gate_ref[0, e:e + 1, :] = res[2:3] + res[3:4] + res[4:5]


def _topk(aff_t, cap):
    b, ne, n = aff_t.shape
    return pl.pallas_call(
        functools.partial(_topk_kernel, cap=cap),
        out_shape=(jax.ShapeDtypeStruct((b, ne, cap), jnp.int32),
                   jax.ShapeDtypeStruct((b, ne, cap), F32),
                   jax.ShapeDtypeStruct((b, ne, n), jnp.int32)),
        grid=(b,),
        in_specs=[pl.BlockSpec((1, ne, n), lambda bb: (bb, 0, 0))],
        out_specs=(pl.BlockSpec((1, ne, cap), lambda bb: (bb, 0, 0)),
                   pl.BlockSpec((1, ne, cap), lambda bb: (bb, 0, 0)),
                   pl.BlockSpec((1, ne, n), lambda bb: (bb, 0, 0))),
        compiler_params=_cparams(("parallel",)),
        name="expert_topk",
    )(aff_t)


def _expert_kernel(idx_ref, hx_hbm, w1_ref, w3_ref, w2_ref, gate_ref, g2_ref, ye_ref, xg, sem, *, n, cap):
    e = pl.program_id(0)
    b = pl.program_id(1)
    base = (b * N_EXPERTS + e) * cap

    def row_copy(r, src_row):
        return pltpu.make_async_copy(hx_hbm.at[pl.ds(src_row, 1)], xg.at[pl.ds(r, 1)], sem)

    def issue(r, carry):
        row_copy(r, b * n + idx_ref[base + r]).start()
        return carry

    def wait(r, carry):
        row_copy(r, 0).wait()
        return carry

    lax.fori_loop(0, cap, issue, 0)
    lax.fori_loop(0, cap, wait, 0)
    x = xg[...].astype(BF16)
    hid = _silu(_dot(x, w1_ref[...])) * _dot(x, w3_ref[...])
    y = _dot(hid.astype(BF16), w2_ref[...])
    ye_ref[0, 0] = y * gate_ref[0, 0] * g2_ref[0]


def _experts(idx_flat, hx2, w1_all, w3_all, w2_all, gate_col, mod3, l, b, n, cap):
    return pl.pallas_call(
        functools.partial(_expert_kernel, n=n, cap=cap),
        out_shape=jax.ShapeDtypeStruct((b, N_EXPERTS, cap, D_MODEL), F32),
        grid_spec=pltpu.PrefetchScalarGridSpec(
            num_scalar_prefetch=1,
            grid=(N_EXPERTS, b),
            in_specs=[
                pl.BlockSpec(memory_space=pl.ANY),
                pl.BlockSpec((None, None, D_MODEL, EXPERT_FF), lambda e, bb, idx: (l, e, 0, 0)),
                pl.BlockSpec((None, None, D_MODEL, EXPERT_FF), lambda e, bb, idx: (l, e, 0, 0)),
                pl.BlockSpec((None, None, EXPERT_FF, D_MODEL), lambda e, bb, idx: (l, e, 0, 0)),
                pl.BlockSpec((1, 1, cap, 1), lambda e, bb, idx: (bb, e, 0, 0)),
                pl.BlockSpec((1, 1, D_MODEL), lambda e, bb, idx: (bb, 0, 5)),
            ],
            out_specs=pl.BlockSpec((1, 1, cap, D_MODEL), lambda e, bb, idx: (bb, e, 0, 0)),
            scratch_shapes=[pltpu.VMEM((cap, D_MODEL), F32), pltpu.SemaphoreType.DMA],
        ),
        compiler_params=_cparams(("arbitrary", "arbitrary")),
        name="expert_ffn",
    )(idx_flat, hx2, w1_all, w3_all, w2_all, gate_col, mod3)


def _scatter_kernel(idx_ref, bnd_ref, x_ref, ye_ref, o_ref, *, cap, nq, rows):
    b = pl.program_id(0)
    q = pl.program_id(1)
    e = pl.program_id(2)

    @pl.when(e == 0)
    def _():
        o_ref[...] = x_ref[...]

    be = b * N_EXPERTS + e
    lo = bnd_ref[be * (nq + 1) + q]
    hi = bnd_ref[be * (nq + 1) + q + 1]

    def body(r, carry):
        t = idx_ref[be * cap + r] - q * rows
        o_ref[0, pl.ds(t, 1), :] = o_ref[0, pl.ds(t, 1), :] + ye_ref[0, 0, pl.ds(r, 1), :]
        return carry

    lax.fori_loop(lo, hi, body, 0)


def _scatter_add(idx_flat, bnd_flat, x3, ye, cap, nq):
    b, n, _ = x3.shape
    rows = n // nq
    return pl.pallas_call(
        functools.partial(_scatter_kernel, cap=cap, nq=nq, rows=rows),
        out_shape=jax.ShapeDtypeStruct(x3.shape, F32),
        grid_spec=pltpu.PrefetchScalarGridSpec(
            num_scalar_prefetch=2,
            grid=(b, nq, N_EXPERTS),
            in_specs=[
                pl.BlockSpec((1, rows, D_MODEL), lambda bb, q, e, idx, bnd: (bb, q, 0)),
                pl.BlockSpec((1, 1, cap, D_MODEL), lambda bb, q, e, idx, bnd: (bb, e, 0, 0)),
            ],
            out_specs=pl.BlockSpec((1, rows, D_MODEL), lambda bb, q, e, idx, bnd: (bb, q, 0)),
        ),
        compiler_params=_cparams(("parallel", "parallel", "arbitrary")),
        name="moe_scatter_add",
    )(idx_flat, bnd_flat, x3, ye)


def _rope_tables(n, use_rope):
    ones_nope = jnp.ones((n, QK_NOPE), F32)
    pad = jnp.zeros((n, LANES - QK_HEAD), F32)
    if not use_rope:
        z = jnp.zeros((n, LANES), F32)
        return jnp.concatenate([ones_nope, jnp.ones((n, QK_ROPE), F32), pad], axis=1), z, z
    rows = n // GRID_W
    t_row = jnp.repeat(jnp.arange(rows, dtype=F32), GRID_W)
    t_col = jnp.tile(jnp.arange(GRID_W, dtype=F32), rows)
    inv = ROPE_THETA ** (-jnp.arange(ROPE_PAIRS, dtype=F32) / ROPE_PAIRS)
    ang = jnp.stack([t_row[:, None] * inv, t_col[:, None] * inv], axis=1)
    cos, sin = jnp.cos(ang), jnp.sin(ang)
    z8 = jnp.zeros_like(sin)
    c = jnp.stack([cos, cos], axis=2).reshape(n, QK_ROPE)
    s1 = jnp.stack([-sin, z8], axis=2).reshape(n, QK_ROPE)
    s2 = jnp.stack([z8, sin], axis=2).reshape(n, QK_ROPE)
    zn = jnp.zeros((n, QK_NOPE), F32)
    return (jnp.concatenate([ones_nope, c, pad], axis=1),
            jnp.concatenate([zn, s1, pad], axis=1),
            jnp.concatenate([zn, s2, pad], axis=1))


def _dft_tables(size, split):
    j = jnp.arange(size, dtype=jnp.int32)[:, None]
    na = size // split
    a = jnp.arange(na, dtype=jnp.int32)[None, :]
    bb = jnp.arange(split, dtype=jnp.int32)[None, :]
    ang_a = (2.0 * math.pi / na) * ((j * a) % na).astype(F32)
    ang_b = (2.0 * math.pi / size) * ((j * bb) % size).astype(F32)
    ca, sa, cb, sb = jnp.cos(ang_a), jnp.sin(ang_a), jnp.cos(ang_b), jnp.sin(ang_b)
    cos = (ca[:, :, None] * cb[:, None, :] - sa[:, :, None] * sb[:, None, :]).reshape(size, size)
    sin = (sa[:, :, None] * cb[:, None, :] + ca[:, :, None] * sb[:, None, :]).reshape(size, size)
    return cos, sin


def _pos_dft_table(n):
    split = 1
    while split * split < n:
        split *= 2
    cos, sin = _dft_tables(n, split)
    return (jnp.concatenate([cos, sin], axis=1) * (n ** -0.5)).astype(BF16)


def _chan_dft_table():
    cos, sin = _dft_tables(FNET_GW, 16)
    eye = jnp.eye(FNET_GROUPS, dtype=F32)
    scale = FNET_GW ** -0.5
    return jnp.concatenate([jnp.kron(eye, cos * scale), jnp.kron(eye, -sin * scale)], axis=1).astype(BF16)


def _pad_heads(w, width):
    nl, k, _ = w.shape
    w = w.reshape(nl, k, MLA_HEADS, width)
    return jnp.pad(w, ((0, 0), (0, 0), (0, 0), (0, LANES - width))).reshape(nl, k, MLA_HEADS * LANES)


def _prep_weights(w_in, w_uq, w_uk, w_uv, g_qn, g_kn, w_gate, w_branch, w_out, w_router, w_e1, w_e3, w_e2):
    nl = w_in.shape[0]
    z = lambda w: jnp.zeros((nl, D_MODEL, w), F32)
    w_in_p = jnp.concatenate([
        w_in[:, :, :OFF_KR], z(QK_NOPE), w_in[:, :, OFF_KR:OFF_CONF], z(2 * COL_BLK - OFF_KR - QK_HEAD),
        w_in[:, :, OFF_CONF:]], axis=2).astype(BF16)
    w_uv_h = w_uv.reshape(nl, KV_LORA, MLA_HEADS // 2, 2, V_HEAD)
    zv = jnp.zeros_like(w_uv_h[:, :, :, 0])
    w_uv_p = jnp.stack([jnp.concatenate([w_uv_h[:, :, :, 0], zv], axis=-1),
                        jnp.concatenate([zv, w_uv_h[:, :, :, 1]], axis=-1)], axis=3)
    padg = lambda g: jnp.pad(g, ((0, 0), (0, LANES - QK_HEAD))).reshape(nl, 1, LANES)
    return dict(
        w_in=w_in_p,
        w_uq=_pad_heads(w_uq, QK_HEAD).astype(BF16),
        w_uk=_pad_heads(w_uk, QK_NOPE).astype(BF16),
        w_uv=w_uv_p.reshape(nl, KV_LORA, MLA_HEADS * LANES).astype(BF16),
        g_qn=padg(g_qn), g_kn=padg(g_kn),
        w_gate=w_gate.astype(BF16), w_branch=w_branch.astype(BF16), w_out=w_out.astype(BF16),
        w_router_t=jnp.swapaxes(w_router, 1, 2),
        w_e1=w_e1.astype(BF16), w_e3=w_e3.astype(BF16), w_e2=w_e2.astype(BF16),
    )


def _mix_inputs(x2, mod3, p, g_mix, l, n):
    return _in_proj(x2, g_mix, mod3, p["w_in"], l, n)


def _keys_values(u, p, g_kv_lora, rope, l, b, n):
    return _kv_proj(u, g_kv_lora, p["w_uk"], p["w_uv"], p["g_kn"], rope, l, b, n)


def _mix_finish(x2, u, hx, att, mod3, p, loc, t_pos, w_chan, l, b, n):
    m = x2.shape[0]
    y_conf, y_sc = _local_branches(u, loc["conf_dw"], loc["conf_ln_g"], loc["conf_ln_b"], loc["sc_dw"], l, n)
    ab = _chan_dft(u, w_chan, b, n)
    y_f = _pos_dft(t_pos, ab.reshape(b, 2 * n, COL_BLK)).reshape(m, COL_BLK)
    mm = _merge(hx, (att.reshape(m, BRANCH_W), y_conf, y_sc, y_f), p["w_gate"], p["w_branch"], l)
    return _out_proj(mm, p["w_out"], x2, mod3, l, n)


def _moe(x2, mod3, p, g_ffn, l, b, n, nq):
    cap = EC_CAPACITY_FACTOR * n // N_EXPERTS
    hx2, aff_t = _router(x2, g_ffn, mod3, p["w_router_t"], l, b, n)
    idx, gate, cnt = _topk(aff_t, cap)
    rows = n // nq
    bnd = jnp.concatenate([jnp.zeros((b, N_EXPERTS, 1), jnp.int32), cnt[:, :, rows - 1::rows]], axis=2)
    idx_flat = idx.reshape(-1)
    ye = _experts(idx_flat, hx2, p["w_e1"], p["w_e3"], p["w_e2"], gate.reshape(b, N_EXPERTS, cap, 1), mod3,
                  l, b, n, cap)
    out = _scatter_add(idx_flat, bnd.reshape(-1), x2.reshape(b, n, D_MODEL), ye, cap, nq)
    return out.reshape(b * n, D_MODEL)


def kernel(x, c, ctx, c_ctx, w_mod, b_mod, g_mix, g_ffn, w_in, g_q_lora, w_uq, g_kv_lora, w_uk, w_uv,
           g_qn, g_kn, conf_dw, conf_ln_g, conf_ln_b, sc_dw, w_branch, w_gate, w_out, w_router,
           w_e1, w_e3, w_e2):
    b, n, _ = x.shape
    nc = ctx.shape[1]
    nl = w_mod.shape[0]
    p = _prep_weights(w_in, w_uq, w_uk, w_uv, g_qn, g_kn, w_gate, w_branch, w_out, w_router, w_e1, w_e3, w_e2)
    r3 = lambda g: g.reshape(nl, 1, g.shape[-1])
    g_mix3, g_ffn3, g_ql3, g_kvl3 = r3(g_mix), r3(g_ffn), r3(g_q_lora), r3(g_kv_lora)
    loc = dict(conf_dw=conf_dw, conf_ln_g=r3(conf_ln_g), conf_ln_b=r3(conf_ln_b), sc_dw=sc_dw)
    rope_x = _rope_tables(n, True)
    rope_c = _rope_tables(nc, False)
    t_pos_x = _pos_dft_table(n)
    t_pos_c = _pos_dft_table(nc)
    w_chan = _chan_dft_table()

    cond = jnp.concatenate([c, c_ctx[None, :], jnp.zeros((8 - b - 1, D_MODEL), F32)], axis=0)
    mods = _mod_all(cond, w_mod, b_mod)

    x2 = x.reshape(b * n, D_MODEL)
    xc2 = ctx.reshape(b * nc, D_MODEL)
    for l in range(nl):
        last = l == nl - 1
        mod_x = mods[l, :b].reshape(b, 1, 6 * D_MODEL)
        mod_c = jnp.broadcast_to(mods[l, b].reshape(1, 1, 6 * D_MODEL), (b, 1, 6 * D_MODEL))

        u_x, hx = _mix_inputs(x2, mod_x, p, g_mix3, l, n)
        u_c, hc = _mix_inputs(xc2, mod_c, p, g_mix3, l, nc)
        k_c, v_c = _keys_values(u_c, p, g_kvl3, rope_c, l, b, nc)
        k_x, v_x = _keys_values(u_x, p, g_kvl3, rope_x, l, b, n)
        q_x = _q_proj(u_x, g_ql3, p["w_uq"], p["g_qn"], rope_x, l, b, n)
        att_x = _attention(q_x, (k_c, v_c, k_x, v_x))
        x2 = _mix_finish(x2, u_x, hx, att_x, mod_x, p, loc, t_pos_x, w_chan, l, b, n)
        if not last:
            q_c = _q_proj(u_c, g_ql3, p["w_uq"], p["g_qn"], rope_c, l, b, nc)
            att_c = _attention(q_c, (k_c, v_c))
            xc2 = _mix_finish(xc2, u_c, hc, att_c, mod_c, p, loc, t_pos_c, w_chan, l, b, nc)

        x2 = _moe(x2, mod_x, p, g_ffn3, l, b, n, 4 if n >= 1024 else 1)
        if not last:
            xc2 = _moe(xc2, mod_c, p, g_ffn3, l, b, nc, 1)
    return x2.reshape(b, n, D_MODEL)
```

```python
import functools
import math

import jax
import jax.numpy as jnp
from jax import lax
from jax.experimental import pallas as pl
from jax.experimental.pallas import tpu as pltpu

F32 = jnp.float32
BF16 = jnp.bfloat16

D_MODEL = 2048
DEPTH = 4
GRID_W = 64
N_BRANCH = 4
BRANCH_W = 512
MLA_HEADS = 8
V_HEAD = 64
QK_NOPE = 64
QK_ROPE = 32
QK_HEAD = 96
ROPE_PAIRS = 8
Q_LORA = 512
KV_LORA = 256
ROPE_THETA = 10000.0
CONF_K = 31
SC_K = 3
FNET_GROUPS = 4
FNET_GW = 128
N_EXPERTS = 16
EXPERT_FF = 1024
EC_CAPACITY_FACTOR = 2
EPS = 1e-6
OFF_KR = 768
OFF_CONF = 800

LANES = 128
SUBLANES = 8
HALO = 16
IN_PAD_W = 4096
COL_BLK = 512
VMEM_LIMIT = 56 * 1024 * 1024

Q_SCALE = QK_HEAD ** -0.5 * math.log2(math.e)
_NT = (((1,), (1,)), ((), ()))


def _cparams(sem):
    return pltpu.CompilerParams(dimension_semantics=sem, vmem_limit_bytes=VMEM_LIMIT)


def _dot(a, b):
    return jnp.dot(a, b, preferred_element_type=F32)


def _dot_nt(a, b):
    return lax.dot_general(a, b, _NT, preferred_element_type=F32)


def _split2(a):
    hi = a.astype(BF16)
    lo = (a - hi.astype(F32)).astype(BF16)
    return hi, lo


def _dot3(a, b, nt=False):
    f = _dot_nt if nt else _dot
    ah, al = _split2(a)
    bh, bl = _split2(b)
    return f(ah, bh) + (f(al, bh) + f(ah, bl))


def _rms(x, g, width):
    ms = jnp.sum(x * x, axis=-1, keepdims=True) * (1.0 / width)
    return x * lax.rsqrt(ms + EPS) * g


def _silu(x):
    return x * jax.nn.sigmoid(x)


def _mod_kernel(s_ref, w_ref, b_ref, o_ref):
    s = _silu(s_ref[...])
    o_ref[...] = _dot3(s, w_ref[...]) + b_ref[...]


def _mod_all(cond, w_mod, b_mod):
    nl = w_mod.shape[0]
    tn = 1024
    return pl.pallas_call(
        _mod_kernel,
        out_shape=jax.ShapeDtypeStruct((nl, 8, 6 * D_MODEL), F32),
        grid=(nl, 6 * D_MODEL // tn),
        in_specs=[
            pl.BlockSpec((8, D_MODEL), lambda l, j: (0, 0)),
            pl.BlockSpec((None, D_MODEL, tn), lambda l, j: (l, 0, j)),
            pl.BlockSpec((None, 1, tn), lambda l, j: (l, 0, j)),
        ],
        out_specs=pl.BlockSpec((None, 8, tn), lambda l, j: (l, 0, j)),
        compiler_params=_cparams(("parallel", "parallel")),
        name="adaln_mod",
    )(cond, w_mod, b_mod.reshape(nl, 1, 6 * D_MODEL))


def _in_proj_kernel(x_ref, g_ref, sh_ref, sc_ref, w_ref, u_ref, hx_ref, *, chunk):
    @pl.when(pl.program_id(1) == 0)
    def _():
        g, sc, sh = g_ref[...], 1.0 + sc_ref[0], sh_ref[0]
        for r0 in range(0, x_ref.shape[0], chunk):
            y = _rms(x_ref[r0:r0 + chunk, :], g, D_MODEL)
            hx_ref[r0:r0 + chunk, :] = (y * sc + sh).astype(BF16)

    u_ref[...] = _dot(hx_ref[...], w_ref[...]).astype(BF16)


def _in_proj(x2, g_all, mod3, w_all, l, n):
    m = x2.shape[0]
    tm = min(1024, n)
    tpb = n // tm
    tn = 1024
    return pl.pallas_call(
        functools.partial(_in_proj_kernel, chunk=256),
        out_shape=(jax.ShapeDtypeStruct((m, IN_PAD_W), BF16), jax.ShapeDtypeStruct((m, D_MODEL), BF16)),
        grid=(m // tm, IN_PAD_W // tn),
        in_specs=[
            pl.BlockSpec((tm, D_MODEL), lambda i, j: (i, 0)),
            pl.BlockSpec((None, 1, D_MODEL), lambda i, j: (l, 0, 0)),
            pl.BlockSpec((1, 1, D_MODEL), lambda i, j: (i // tpb, 0, 0)),
            pl.BlockSpec((1, 1, D_MODEL), lambda i, j: (i // tpb, 0, 1)),
            pl.BlockSpec((None, D_MODEL, tn), lambda i, j: (l, 0, j)),
        ],
        out_specs=(
            pl.BlockSpec((tm, tn), lambda i, j: (i, j)),
            pl.BlockSpec((tm, D_MODEL), lambda i, j: (i, 0)),
        ),
        compiler_params=_cparams(("parallel", "arbitrary")),
        name="norm_in_proj",
    )(x2, g_all, mod3, mod3, w_all)


def _head_norm_rope(xh, gn, c, s1, s2):
    xh = _rms(xh, gn, QK_HEAD)
    return xh * c + pltpu.roll(xh, LANES - ROPE_PAIRS, 1) * s1 + pltpu.roll(xh, ROPE_PAIRS, 1) * s2


def _q_kernel(cq_ref, gl_ref, w_ref, gn_ref, c_ref, s1_ref, s2_ref, q_ref):
    y = _rms(cq_ref[...].astype(F32), gl_ref[...], Q_LORA).astype(BF16)
    q = _dot(y, w_ref[...])
    gn = gn_ref[...]
    c, s1, s2 = c_ref[...], s1_ref[...], s2_ref[...]
    for h in range(MLA_HEADS):
        qh = _head_norm_rope(q[:, h * LANES:(h + 1) * LANES], gn, c, s1, s2)
        q_ref[0, h] = (qh * Q_SCALE).astype(BF16)


def _q_proj(u, gl_all, w_all, gn_all, rope, l, b, n):
    m = u.shape[0]
    tm = min(512, n)
    tpb = n // tm
    tab = pl.BlockSpec((tm, LANES), lambda i: (i % tpb, 0))
    return pl.pallas_call(
        _q_kernel,
        out_shape=jax.ShapeDtypeStruct((b, MLA_HEADS, n, LANES), BF16),
        grid=(m // tm,),
        in_specs=[
            pl.BlockSpec((tm, COL_BLK), lambda i: (i, 0)),
            pl.BlockSpec((None, 1, Q_LORA), lambda i: (l, 0, 0)),
            pl.BlockSpec((None, Q_LORA, MLA_HEADS * LANES), lambda i: (l, 0, 0)),
            pl.BlockSpec((None, 1, LANES), lambda i: (l, 0, 0)),
            tab, tab, tab,
        ],
        out_specs=pl.BlockSpec((1, MLA_HEADS, tm, LANES), lambda i: (i // tpb, 0, i % tpb, 0)),
        compiler_params=_cparams(("parallel",)),
        name="mla_q",
    )(u, gl_all, w_all, gn_all, *rope)


def _kv_kernel(ck_ref, gl_ref, wk_ref, wv_ref, gn_ref, c_ref, s1_ref, s2_ref, k_ref, v_ref):
    blk = ck_ref[...].astype(F32)
    y = _rms(blk[:, :KV_LORA], gl_ref[...], KV_LORA).astype(BF16)
    kn = _dot(y, wk_ref[...])
    v = _dot(y, wv_ref[...])
    kr = blk[:, KV_LORA:KV_LORA + LANES]
    gn = gn_ref[...]
    c, s1, s2 = c_ref[...], s1_ref[...], s2_ref[...]
    for h in range(MLA_HEADS):
        kh = _head_norm_rope(kn[:, h * LANES:(h + 1) * LANES] + kr, gn, c, s1, s2)
        k_ref[0, h] = kh.astype(BF16)
        v_ref[0, h] = v[:, h * LANES:(h + 1) * LANES].astype(BF16)


def _kv_proj(u, gl_all, wk_all, wv_all, gn_all, rope, l, b, n):
    m = u.shape[0]
    tm = min(512, n)
    tpb = n // tm
    tab = pl.BlockSpec((tm, LANES), lambda i: (i % tpb, 0))
    hw = MLA_HEADS * LANES
    o_spec = pl.BlockSpec((1, MLA_HEADS, tm, LANES), lambda i: (i // tpb, 0, i % tpb, 0))
    o_shape = jax.ShapeDtypeStruct((b, MLA_HEADS, n, LANES), BF16)
    return pl.pallas_call(
        _kv_kernel,
        out_shape=(o_shape, o_shape),
        grid=(m // tm,),
        in_specs=[
            pl.BlockSpec((tm, COL_BLK), lambda i: (i, 1)),
            pl.BlockSpec((None, 1, KV_LORA), lambda i: (l, 0, 0)),
            pl.BlockSpec((None, KV_LORA, hw), lambda i: (l, 0, 0)),
            pl.BlockSpec((None, KV_LORA, hw), lambda i: (l, 0, 0)),
            pl.BlockSpec((None, 1, LANES), lambda i: (l, 0, 0)),
            tab, tab, tab,
        ],
        out_specs=(o_spec, o_spec),
        compiler_params=_cparams(("parallel",)),
        name="mla_kv",
    )(u, gl_all, wk_all, wv_all, gn_all, *rope)


def _attn_kernel(*refs, n_src, key_chunk, q_rows):
    q_ref = refs[0]
    kv = refs[1:1 + 2 * n_src]
    o_ref = refs[1 + 2 * n_src]
    scr = refs[2 + 2 * n_src:]
    chunks = []
    col = 0
    for si in range(n_src):
        nk = kv[2 * si].shape[2]
        for st in range(0, nk, key_chunk):
            sz = min(key_chunk, nk - st)
            chunks.append((si, st, sz, col))
            col += sz
    units = [(hh, r0) for r0 in range(0, q_ref.shape[2], q_rows) for hh in range(2)]

    def scores(u):
        hh, r0 = units[u]
        q = q_ref[0, hh, r0:r0 + q_rows, :]
        mx = None
        for si, st, sz, co in chunks:
            s = _dot_nt(q, kv[2 * si][0, hh, st:st + sz, :])
            scr[u % len(scr)][:, co:co + sz] = s
            cm = jnp.max(s, axis=-1, keepdims=True)
            mx = cm if mx is None else jnp.maximum(mx, cm)
        return mx

    def weighted(u, mx):
        hh, _ = units[u]
        den = None
        acc = None
        for si, st, sz, co in chunks:
            p = jnp.exp2(scr[u % len(scr)][:, co:co + sz] - mx)
            ps = jnp.sum(p, axis=-1, keepdims=True)
            den = ps if den is None else den + ps
            pv = _dot(p.astype(BF16), kv[2 * si + 1][0, hh, st:st + sz, :])
            acc = pv if acc is None else acc + pv
        return acc / den

    outs = []
    mx_prev = scores(0)
    for u in range(1, len(units)):
        mx = scores(u)
        outs.append(weighted(u - 1, mx_prev))
        mx_prev = mx
    outs.append(weighted(len(units) - 1, mx_prev))
    for i in range(len(units) // 2):
        o_ref[0, i * q_rows:(i + 1) * q_rows, :] = (outs[2 * i] + outs[2 * i + 1]).astype(BF16)


def _attention(q, kvs):
    b, _, nq, _ = q.shape
    tq = min(512, nq)
    q_rows = min(256, tq)
    n_src = len(kvs) // 2
    nk_tot = sum(kvs[2 * i].shape[2] for i in range(n_src))
    n_scr = min(3, 2 * tq // q_rows)
    in_specs = [pl.BlockSpec((1, 2, tq, LANES), lambda bb, hp, i: (bb, hp, i, 0))]
    for a in kvs:
        in_specs.append(pl.BlockSpec((1, 2, a.shape[2], LANES), lambda bb, hp, i: (bb, hp, 0, 0)))
    return pl.pallas_call(
        functools.partial(_attn_kernel, n_src=n_src, key_chunk=1024, q_rows=q_rows),
        out_shape=jax.ShapeDtypeStruct((b, nq, BRANCH_W), BF16),
        grid=(b, MLA_HEADS // 2, nq // tq),
        in_specs=in_specs,
        out_specs=pl.BlockSpec((1, tq, LANES), lambda bb, hp, i: (bb, i, hp)),
        scratch_shapes=[pltpu.VMEM((q_rows, nk_tot), F32) for _ in range(n_scr)],
        compiler_params=_cparams(("parallel", "parallel", "parallel")),
        name="mla_attention",
    )(q, *kvs)


def _local_kernel(a_ref, g_ref, sb_ref, scc_ref, sx_ref,
                  ap_ref, gp_ref, cp_ref, xp_ref, an_ref, gn_ref, cn_ref, xn_ref,
                  cw_ref, lg_ref, lb_ref, sw_ref, y_ref, z_ref, g_scr, h_scr, *, tpb, rows):
    i = pl.program_id(0)
    tm = a_ref.shape[0]
    has_prev = (i % tpb) > 0
    has_next = (i % tpb) < tpb - 1
    zero = jnp.zeros((HALO, COL_BLK), F32)
    f = lambda ref: ref[...].astype(F32)

    def fill(scr, slot, sh, prev, curr, nxt):
        scr[slot, 0:HALO - sh, :] = prev[sh:, :]
        scr[slot, HALO - sh:HALO - sh + tm, :] = curr
        scr[slot, HALO - sh + tm:2 * HALO - sh + tm, :] = nxt

    gp = jnp.where(has_prev, f(ap_ref) * jax.nn.sigmoid(f(gp_ref)), zero)
    gc = f(a_ref) * jax.nn.sigmoid(f(g_ref))
    gn = jnp.where(has_next, f(an_ref) * jax.nn.sigmoid(f(gn_ref)), zero)
    for sh in range(SUBLANES):
        fill(g_scr, sh, sh, gp, gc, gn)
    sc_offs = [HALO - SC_K // 2 + j for j in range(SC_K)]
    hp = jnp.where(has_prev, f(cp_ref) * f(xp_ref), zero)
    hc = f(scc_ref) * f(sx_ref)
    hn = jnp.where(has_next, f(cn_ref) * f(xn_ref), zero)
    for j, o in enumerate(sc_offs):
        fill(h_scr, j, o % SUBLANES, hp, hc, hn)
    lg, lb = lg_ref[...], lb_ref[...]
    for c in range(tm // rows):
        r0 = c * rows
        acc = None
        for j in range(CONF_K):
            o = HALO - CONF_K // 2 + j
            st = r0 + o - o % SUBLANES
            t = cw_ref[j:j + 1, :] * g_scr[o % SUBLANES, st:st + rows, :]
            acc = t if acc is None else acc + t
        mu = jnp.mean(acc, axis=-1, keepdims=True)
        dv = acc - mu
        var = jnp.mean(dv * dv, axis=-1, keepdims=True)
        y = dv * lax.rsqrt(var + EPS) * lg + lb
        y_ref[r0:r0 + rows, :] = _silu(y).astype(BF16)
        acc = None
        for j, o in enumerate(sc_offs):
            st = r0 + o - o % SUBLANES
            t = sw_ref[j:j + 1, :] * h_scr[j, st:st + rows, :]
            acc = t if acc is None else acc + t
        z_ref[r0:r0 + rows, :] = (sb_ref[r0:r0 + rows, :].astype(F32) * acc).astype(BF16)


def _local_branches(u, cw_all, lg_all, lb_all, sw_all, l, n):
    m = u.shape[0]
    tm = min(512, n)
    tpb = n // tm
    hb = tm // HALO
    last = m // HALO - 1

    def cur(cb):
        return pl.BlockSpec((tm, COL_BLK), lambda i: (i, cb))

    def prev(cb):
        return pl.BlockSpec((HALO, COL_BLK), lambda i: (jnp.maximum(i * hb - 1, 0), cb))

    def nxt(cb):
        return pl.BlockSpec((HALO, COL_BLK), lambda i: (jnp.minimum((i + 1) * hb, last), cb))

    o_shape = jax.ShapeDtypeStruct((m, COL_BLK), BF16)
    o_spec = pl.BlockSpec((tm, COL_BLK), lambda i: (i, 0))
    return pl.pallas_call(
        functools.partial(_local_kernel, tpb=tpb, rows=32),
        out_shape=(o_shape, o_shape),
        grid=(m // tm,),
        in_specs=[
            cur(2), cur(3), cur(4), cur(5), cur(6),
            prev(2), prev(3), prev(5), prev(6),
            nxt(2), nxt(3), nxt(5), nxt(6),
            pl.BlockSpec((None, CONF_K, COL_BLK), lambda i: (l, 0, 0)),
            pl.BlockSpec((None, 1, COL_BLK), lambda i: (l, 0, 0)),
            pl.BlockSpec((None, 1, COL_BLK), lambda i: (l, 0, 0)),
            pl.BlockSpec((None, SC_K, COL_BLK), lambda i: (l, 0, 0)),
        ],
        out_specs=(o_spec, o_spec),
        scratch_shapes=[pltpu.VMEM((SUBLANES, tm + 2 * HALO, COL_BLK), F32),
                        pltpu.VMEM((SC_K, tm + 2 * HALO, COL_BLK), F32)],
        compiler_params=_cparams(("parallel",)),
        name="local_convs",
    )(u, u, u, u, u, u, u, u, u, u, u, u, u, cw_all, lg_all, lb_all, sw_all)


def _chan_dft_kernel(f_ref, w_ref, o_ref):
    r = _dot(f_ref[...], w_ref[...])
    o_ref[0, 0] = r[:, :COL_BLK].astype(BF16)
    o_ref[0, 1] = r[:, COL_BLK:].astype(BF16)


def _chan_dft(u, w_chan, b, n):
    m = u.shape[0]
    tm = min(512, n)
    tpb = n // tm
    return pl.pallas_call(
        _chan_dft_kernel,
        out_shape=jax.ShapeDtypeStruct((b, 2, n, COL_BLK), BF16),
        grid=(m // tm,),
        in_specs=[
            pl.BlockSpec((tm, COL_BLK), lambda i: (i, 7)),
            pl.BlockSpec((COL_BLK, 2 * COL_BLK), lambda i: (0, 0)),
        ],
        out_specs=pl.BlockSpec((1, 2, tm, COL_BLK), lambda i: (i // tpb, 0, i % tpb, 0)),
        compiler_params=_cparams(("parallel",)),
        name="fnet_chan_dft",
    )(u, w_chan)


def _pos_dft_kernel(t_ref, ab_ref, o_ref, acc_ref):
    k = pl.program_id(2)

    @pl.when(k == 0)
    def _():
        acc_ref[...] = jnp.zeros_like(acc_ref)

    acc_ref[...] += _dot(t_ref[...], ab_ref[0])

    @pl.when(k == pl.num_programs(2) - 1)
    def _():
        o_ref[0] = acc_ref[...].astype(BF16)


def _pos_dft(t_pos, ab):
    b, n2, _ = ab.shape
    n = n2 // 2
    tm = min(1024, n)
    tk = min(2048, n2)
    return pl.pallas_call(
        _pos_dft_kernel,
        out_shape=jax.ShapeDtypeStruct((b, n, COL_BLK), BF16),
        grid=(b, n // tm, n2 // tk),
        in_specs=[
            pl.BlockSpec((tm, tk), lambda bb, i, k: (i, k)),
            pl.BlockSpec((1, tk, COL_BLK), lambda bb, i, k: (bb, k, 0)),
        ],
        out_specs=pl.BlockSpec((1, tm, COL_BLK), lambda bb, i, k: (bb, i, 0)),
        scratch_shapes=[pltpu.VMEM((tm, COL_BLK), F32)],
        compiler_params=_cparams(("parallel", "parallel", "arbitrary")),
        name="fnet_pos_dft",
    )(t_pos, ab)


def _merge_kernel(hx_ref, y0_ref, y1_ref, y2_ref, y3_ref, wg0_ref, wg1_ref, wg2_ref, wg3_ref, wb_ref, o_ref):
    hx = hx_ref[...]
    ys = (y0_ref, y1_ref, y2_ref, y3_ref)
    wgs = (wg0_ref, wg1_ref, wg2_ref, wg3_ref)
    acc = None
    for k in range(N_BRANCH):
        t = jax.nn.sigmoid(_dot(hx, wgs[k][...])) * _dot(ys[k][...], wb_ref[k])
        acc = t if acc is None else acc + t
    o_ref[...] = acc.astype(BF16)


def _merge(hx, ys, wg_all, wb_all, l):
    m = hx.shape[0]
    tm = min(1024, m)
    tn = 512
    nj = D_MODEL // tn
    y_spec = pl.BlockSpec((tm, BRANCH_W), lambda i, j: (i, 0))
    wg_specs = [pl.BlockSpec((None, D_MODEL, tn), lambda i, j, k=k: (l, 0, k * nj + j)) for k in range(N_BRANCH)]
    return pl.pallas_call(
        _merge_kernel,
        out_shape=jax.ShapeDtypeStruct((m, D_MODEL), BF16),
        grid=(m // tm, nj),
        in_specs=[pl.BlockSpec((tm, D_MODEL), lambda i, j: (i, 0)), y_spec, y_spec, y_spec, y_spec,
                  *wg_specs,
                  pl.BlockSpec((None, N_BRANCH, BRANCH_W, tn), lambda i, j: (l, 0, 0, j))],
        out_specs=pl.BlockSpec((tm, tn), lambda i, j: (i, j)),
        compiler_params=_cparams(("parallel", "parallel")),
        name="branch_merge",
    )(hx, *ys, wg_all, wg_all, wg_all, wg_all, wb_all)


def _out_proj_kernel(m_ref, w_ref, x_ref, g_ref, o_ref):
    o_ref[...] = x_ref[...] + g_ref[0] * _dot(m_ref[...], w_ref[...])


def _out_proj(mm, w_all, x2, mod3, l, n):
    m = mm.shape[0]
    tm = min(1024, n)
    tpb = n // tm
    tn = 512
    nj = D_MODEL // tn
    return pl.pallas_call(
        _out_proj_kernel,
        out_shape=jax.ShapeDtypeStruct((m, D_MODEL), F32),
        grid=(m // tm, nj),
        in_specs=[
            pl.BlockSpec((tm, D_MODEL), lambda i, j: (i, 0)),
            pl.BlockSpec((None, D_MODEL, tn), lambda i, j: (l, 0, j)),
            pl.BlockSpec((tm, tn), lambda i, j: (i, j)),
            pl.BlockSpec((1, 1, tn), lambda i, j: (i // tpb, 0, 2 * nj + j)),
        ],
        out_specs=pl.BlockSpec((tm, tn), lambda i, j: (i, j)),
        compiler_params=_cparams(("parallel", "parallel")),
        name="out_proj_residual",
    )(mm, w_all, x2, mod3)


def _router_kernel(x_ref, g_ref, sh_ref, sc_ref, wr_ref, hx_ref, aff_ref):
    y = _rms(x_ref[...], g_ref[...], D_MODEL)
    hx = y * (1.0 + sc_ref[0]) + sh_ref[0]
    hx_ref[...] = hx
    lt = _dot3(wr_ref[...], hx, nt=True)
    e = jnp.exp(lt - jnp.max(lt, axis=0, keepdims=True))
    aff_ref[0] = e / jnp.sum(e, axis=0, keepdims=True)


def _router(x2, g_all, mod3, wr_t_all, l, b, n):
    m = x2.shape[0]
    tm = min(512, n)
    tpb = n // tm
    return pl.pallas_call(
        _router_kernel,
        out_shape=(jax.ShapeDtypeStruct((m, D_MODEL), F32), jax.ShapeDtypeStruct((b, N_EXPERTS, n), F32)),
        grid=(m // tm,),
        in_specs=[
            pl.BlockSpec((tm, D_MODEL), lambda i: (i, 0)),
            pl.BlockSpec((None, 1, D_MODEL), lambda i: (l, 0, 0)),
            pl.BlockSpec((1, 1, D_MODEL), lambda i: (i // tpb, 0, 3)),
            pl.BlockSpec((1, 1, D_MODEL), lambda i: (i // tpb, 0, 4)),
            pl.BlockSpec((None, N_EXPERTS, D_MODEL), lambda i: (l, 0, 0)),
        ],
        out_specs=(
            pl.BlockSpec((tm, D_MODEL), lambda i: (i, 0)),
            pl.BlockSpec((1, N_EXPERTS, tm), lambda i: (i // tpb, 0, i % tpb)),
        ),
        compiler_params=_cparams(("parallel",)),
        name="norm_router",
    )(x2, g_all, mod3, mod3, wr_t_all)


def _prefix_count(mask01, tri):
    n = mask01.shape[1]
    carry = jnp.zeros((mask01.shape[0], 1), F32)
    outs = []
    for c in range(n // LANES):
        loc = _dot(mask01[:, c * LANES:(c + 1) * LANES], tri) + carry
        outs.append(loc)
        carry = loc[:, LANES - 1:LANES]
    return jnp.concatenate(outs, axis=1)


def _topk_kernel(a_ref, idx_ref, gate_ref, *, cap):
    a = a_ref[0]
    ne, n = a.shape
    bits = pltpu.bitcast(a, jnp.int32)

    def search(i, t):
        cand = t | (jnp.int32(1) << (30 - i))
        c = jnp.sum(jnp.where(bits >= cand, 1.0, 0.0), axis=-1, keepdims=True)
        return jnp.where(c >= cap, cand, t)

    thr = lax.fori_loop(0, 31, search, jnp.zeros((ne, 1), jnp.int32))
    gt = bits > thr
    eq = bits == thr
    need = cap - jnp.sum(jnp.where(gt, 1.0, 0.0), axis=-1, keepdims=True)
    row = lax.broadcasted_iota(jnp.int32, (LANES, LANES), 0)
    colm = lax.broadcasted_iota(jnp.int32, (LANES, LANES), 1)
    tri = jnp.where(row <= colm, 1.0, 0.0).astype(BF16)
    eq_rank = _prefix_count(jnp.where(eq, 1.0, 0.0).astype(BF16), tri)
    sel = gt | (eq & (eq_rank <= need))
    cnt = _prefix_count(jnp.where(sel, 1.0, 0.0).astype(BF16), tri)
    slot = jnp.where(sel, cnt - 1.0, -1.0)

    tok = lax.broadcasted_iota(jnp.int32, (1, n), 1)
    t_hi = (tok >> 6).astype(F32)
    t_lo = (tok & 63).astype(F32)
    r_iota = lax.broadcasted_iota(jnp.int32, (cap, 1), 0).astype(F32)
    zeros3 = jnp.zeros((3, n), F32)
    for e in range(ne):
        onehot = jnp.where(slot[e:e + 1, :] == r_iota, 1.0, 0.0).astype(BF16)
        ae = a[e:e + 1, :]
        a1 = ae.astype(BF16).astype(F32)
        a2 = (ae - a1).astype(BF16).astype(F32)
        a3 = ae - a1 - a2
        rows = jnp.concatenate([t_hi, t_lo, a1, a2, a3, zeros3], axis=0).astype(BF16)
        res = _dot_nt(rows, onehot)
        idx_ref[0, e:e + 1, :] = (res[0:1] * 64.0 + res[1:2]).astype(jnp.int32)
        gate_ref[0, e:e + 1, :] = res[2:3] + res[3:4] + res[4:5]


def _topk(aff_t, cap):
    b, ne, n = aff_t.shape
    return pl.pallas_call(
        functools.partial(_topk_kernel, cap=cap),
        out_shape=(jax.ShapeDtypeStruct((b, ne, cap), jnp.int32),
                   jax.ShapeDtypeStruct((b, ne, cap), F32)),
        grid=(b,),
        in_specs=[pl.BlockSpec((1, ne, n), lambda bb: (bb, 0, 0))],
        out_specs=(pl.BlockSpec((1, ne, cap), lambda bb: (bb, 0, 0)),
                   pl.BlockSpec((1, ne, cap), lambda bb: (bb, 0, 0))),
        compiler_params=_cparams(("parallel",)),
        name="expert_topk",
    )(aff_t)


def _expert_kernel(idx_ref, hx_hbm, xin_hbm, w1_ref, w3_ref, w2_ref, gate_ref, g2_ref, xo_hbm,
                   xg, og, sem_g, sem_o, sem_w, *, n, cap, bg, nbg):
    del xin_hbm
    e = pl.program_id(0)
    g = pl.program_id(1)
    s = e * nbg + g
    total = N_EXPERTS * nbg
    slot = s % 2
    rows = bg * cap

    def for_rows(e_, g_, fn):
        for bi in range(bg):
            b_ = g_ * bg + bi
            base = (b_ * N_EXPERTS + e_) * cap

            def body(r, carry, b_=b_, base=base, bi=bi):
                fn(b_ * n + idx_ref[base + r], bi * cap + r)
                return carry

            lax.fori_loop(0, cap, body, 0)

    def start_gather(e_, g_, slot_):
        for_rows(e_, g_, lambda src, dst: pltpu.make_async_copy(
            hx_hbm.at[pl.ds(src, 1)], xg.at[slot_, pl.ds(dst, 1)], sem_g.at[slot_]).start())

    def all_rows(ref):
        return ref.at[pl.ds(0, rows)]

    @pl.when(s == 0)
    def _():
        start_gather(e, g, slot)

    @pl.when(s + 1 < total)
    def _():
        wrap = g == nbg - 1
        start_gather(jnp.where(wrap, e + 1, e), jnp.where(wrap, 0, g + 1), 1 - slot)

    pltpu.make_async_copy(all_rows(hx_hbm), xg.at[slot], sem_g.at[slot]).wait()
    x = xg[slot].astype(BF16)
    hid = (_silu(_dot(x, w1_ref[...])) * _dot(x, w3_ref[...])).astype(BF16)

    @pl.when(s > 0)
    def _():
        pltpu.make_async_copy(og, all_rows(xo_hbm), sem_w).wait()

    for_rows(e, g, lambda src, dst: pltpu.make_async_copy(
        xo_hbm.at[pl.ds(src, 1)], og.at[pl.ds(dst, 1)], sem_o).start())
    y = _dot(hid, w2_ref[...])
    pltpu.make_async_copy(all_rows(xo_hbm), og, sem_o).wait()
    for bi in range(bg):
        sl = slice(bi * cap, (bi + 1) * cap)
        og[sl, :] = og[sl, :] + y[sl, :] * gate_ref[bi, 0] * g2_ref[bi]
    for_rows(e, g, lambda src, dst: pltpu.make_async_copy(
        og.at[pl.ds(dst, 1)], xo_hbm.at[pl.ds(src, 1)], sem_w).start())

    @pl.when(s == total - 1)
    def _():
        pltpu.make_async_copy(og, all_rows(xo_hbm), sem_w).wait()


def _experts(idx_flat, hx2, x2, w1_all, w3_all, w2_all, gate_col, mod3, l, b, n, cap, bg):
    nbg = b // bg
    rows = bg * cap
    return pl.pallas_call(
        functools.partial(_expert_kernel, n=n, cap=cap, bg=bg, nbg=nbg),
        out_shape=jax.ShapeDtypeStruct(x2.shape, F32),
        grid_spec=pltpu.PrefetchScalarGridSpec(
            num_scalar_prefetch=1,
            grid=(N_EXPERTS, nbg),
            in_specs=[
                pl.BlockSpec(memory_space=pl.ANY),
                pl.BlockSpec(memory_space=pl.ANY),
                pl.BlockSpec((None, None, D_MODEL, EXPERT_FF), lambda e, g, idx: (l, e, 0, 0)),
                pl.BlockSpec((None, None, D_MODEL, EXPERT_FF), lambda e, g, idx: (l, e, 0, 0)),
                pl.BlockSpec((None, None, EXPERT_FF, D_MODEL), lambda e, g, idx: (l, e, 0, 0)),
                pl.BlockSpec((bg, 1, cap, 1), lambda e, g, idx: (g, e, 0, 0)),
                pl.BlockSpec((bg, 1, D_MODEL), lambda e, g, idx: (g, 0, 5)),
            ],
            out_specs=pl.BlockSpec(memory_space=pl.ANY),
            scratch_shapes=[pltpu.VMEM((2, rows, D_MODEL), F32), pltpu.VMEM((rows, D_MODEL), F32),
                            pltpu.SemaphoreType.DMA((2,)), pltpu.SemaphoreType.DMA, pltpu.SemaphoreType.DMA],
        ),
        input_output_aliases={2: 0},
        compiler_params=_cparams(("arbitrary", "arbitrary")),
        name="expert_ffn",
    )(idx_flat, hx2, x2, w1_all, w3_all, w2_all, gate_col, mod3)


def _rope_tables(n, use_rope):
    ones_nope = jnp.ones((n, QK_NOPE), F32)
    pad = jnp.zeros((n, LANES - QK_HEAD), F32)
    if not use_rope:
        z = jnp.zeros((n, LANES), F32)
        return jnp.concatenate([ones_nope, jnp.ones((n, QK_ROPE), F32), pad], axis=1), z, z
    rows = n // GRID_W
    t_row = jnp.repeat(jnp.arange(rows, dtype=F32), GRID_W)
    t_col = jnp.tile(jnp.arange(GRID_W, dtype=F32), rows)
    inv = ROPE_THETA ** (-jnp.arange(ROPE_PAIRS, dtype=F32) / ROPE_PAIRS)
    ang = jnp.stack([t_row[:, None] * inv, t_col[:, None] * inv], axis=1)
    cos, sin = jnp.cos(ang), jnp.sin(ang)
    z8 = jnp.zeros_like(sin)
    c = jnp.stack([cos, cos], axis=2).reshape(n, QK_ROPE)
    s1 = jnp.stack([-sin, z8], axis=2).reshape(n, QK_ROPE)
    s2 = jnp.stack([z8, sin], axis=2).reshape(n, QK_ROPE)
    zn = jnp.zeros((n, QK_NOPE), F32)
    return (jnp.concatenate([ones_nope, c, pad], axis=1),
            jnp.concatenate([zn, s1, pad], axis=1),
            jnp.concatenate([zn, s2, pad], axis=1))


def _dft_tables(size, split):
    j = jnp.arange(size, dtype=jnp.int32)[:, None]
    na = size // split
    a = jnp.arange(na, dtype=jnp.int32)[None, :]
    bb = jnp.arange(split, dtype=jnp.int32)[None, :]
    ang_a = (2.0 * math.pi / na) * ((j * a) % na).astype(F32)
    ang_b = (2.0 * math.pi / size) * ((j * bb) % size).astype(F32)
    ca, sa, cb, sb = jnp.cos(ang_a), jnp.sin(ang_a), jnp.cos(ang_b), jnp.sin(ang_b)
    cos = (ca[:, :, None] * cb[:, None, :] - sa[:, :, None] * sb[:, None, :]).reshape(size, size)
    sin = (sa[:, :, None] * cb[:, None, :] + ca[:, :, None] * sb[:, None, :]).reshape(size, size)
    return cos, sin


def _pos_dft_table(n):
    split = 1
    while split * split < n:
        split *= 2
    cos, sin = _dft_tables(n, split)
    return (jnp.concatenate([cos, sin], axis=1) * (n ** -0.5)).astype(BF16)


def _chan_dft_table():
    cos, sin = _dft_tables(FNET_GW, 16)
    eye = jnp.eye(FNET_GROUPS, dtype=F32)
    scale = FNET_GW ** -0.5
    return jnp.concatenate([jnp.kron(eye, cos * scale), jnp.kron(eye, -sin * scale)], axis=1).astype(BF16)


def _pad_heads(w, width):
    nl, k, _ = w.shape
    w = w.reshape(nl, k, MLA_HEADS, width)
    return jnp.pad(w, ((0, 0), (0, 0), (0, 0), (0, LANES - width))).reshape(nl, k, MLA_HEADS * LANES)


def _prep_weights(w_in, w_uq, w_uk, w_uv, g_qn, g_kn, w_gate, w_branch, w_out, w_router, w_e1, w_e3, w_e2):
    nl = w_in.shape[0]
    z = lambda w: jnp.zeros((nl, D_MODEL, w), F32)
    w_in_p = jnp.concatenate([
        w_in[:, :, :OFF_KR], z(QK_NOPE), w_in[:, :, OFF_KR:OFF_CONF], z(2 * COL_BLK - OFF_KR - QK_HEAD),
        w_in[:, :, OFF_CONF:]], axis=2).astype(BF16)
    w_uv_h = w_uv.reshape(nl, KV_LORA, MLA_HEADS // 2, 2, V_HEAD)
    zv = jnp.zeros_like(w_uv_h[:, :, :, 0])
    w_uv_p = jnp.stack([jnp.concatenate([w_uv_h[:, :, :, 0], zv], axis=-1),
                        jnp.concatenate([zv, w_uv_h[:, :, :, 1]], axis=-1)], axis=3)
    padg = lambda g: jnp.pad(g, ((0, 0), (0, LANES - QK_HEAD))).reshape(nl, 1, LANES)
    return dict(
        w_in=w_in_p,
        w_uq=_pad_heads(w_uq, QK_HEAD).astype(BF16),
        w_uk=_pad_heads(w_uk, QK_NOPE).astype(BF16),
        w_uv=w_uv_p.reshape(nl, KV_LORA, MLA_HEADS * LANES).astype(BF16),
        g_qn=padg(g_qn), g_kn=padg(g_kn),
        w_gate=w_gate.astype(BF16), w_branch=w_branch.astype(BF16), w_out=w_out.astype(BF16),
        w_router_t=jnp.swapaxes(w_router, 1, 2),
        w_e1=w_e1.astype(BF16), w_e3=w_e3.astype(BF16), w_e2=w_e2.astype(BF16),
    )


def _mix_inputs(x2, mod3, p, g_mix, l, n):
    return _in_proj(x2, g_mix, mod3, p["w_in"], l, n)


def _keys_values(u, p, g_kv_lora, rope, l, b, n):
    return _kv_proj(u, g_kv_lora, p["w_uk"], p["w_uv"], p["g_kn"], rope, l, b, n)


def _mix_finish(x2, u, hx, att, mod3, p, loc, t_pos, w_chan, l, b, n):
    m = x2.shape[0]
    y_conf, y_sc = _local_branches(u, loc["conf_dw"], loc["conf_ln_g"], loc["conf_ln_b"], loc["sc_dw"], l, n)
    ab = _chan_dft(u, w_chan, b, n)
    y_f = _pos_dft(t_pos, ab.reshape(b, 2 * n, COL_BLK)).reshape(m, COL_BLK)
    mm = _merge(hx, (att.reshape(m, BRANCH_W), y_conf, y_sc, y_f), p["w_gate"], p["w_branch"], l)
    return _out_proj(mm, p["w_out"], x2, mod3, l, n)


def _moe(x2, mod3, p, g_ffn, l, b, n, bg):
    cap = EC_CAPACITY_FACTOR * n // N_EXPERTS
    hx2, aff_t = _router(x2, g_ffn, mod3, p["w_router_t"], l, b, n)
    idx, gate = _topk(aff_t, cap)
    return _experts(idx.reshape(-1), hx2, x2, p["w_e1"], p["w_e3"], p["w_e2"],
                    gate.reshape(b, N_EXPERTS, cap, 1), mod3, l, b, n, cap, bg)


def kernel(x, c, ctx, c_ctx, w_mod, b_mod, g_mix, g_ffn, w_in, g_q_lora, w_uq, g_kv_lora, w_uk, w_uv,
           g_qn, g_kn, conf_dw, conf_ln_g, conf_ln_b, sc_dw, w_branch, w_gate, w_out, w_router,
           w_e1, w_e3, w_e2):
    b, n, _ = x.shape
    nc = ctx.shape[1]
    nl = w_mod.shape[0]
    p = _prep_weights(w_in, w_uq, w_uk, w_uv, g_qn, g_kn, w_gate, w_branch, w_out, w_router, w_e1, w_e3, w_e2)
    r3 = lambda g: g.reshape(nl, 1, g.shape[-1])
    g_mix3, g_ffn3, g_ql3, g_kvl3 = r3(g_mix), r3(g_ffn), r3(g_q_lora), r3(g_kv_lora)
    loc = dict(conf_dw=conf_dw, conf_ln_g=r3(conf_ln_g), conf_ln_b=r3(conf_ln_b), sc_dw=sc_dw)
    rope_x = _rope_tables(n, True)
    rope_c = _rope_tables(nc, False)
    t_pos_x = _pos_dft_table(n)
    t_pos_c = _pos_dft_table(nc)
    w_chan = _chan_dft_table()

    cond = jnp.concatenate([c, c_ctx[None, :], jnp.zeros((8 - b - 1, D_MODEL), F32)], axis=0)
    mods = _mod_all(cond, w_mod, b_mod)

    x2 = x.reshape(b * n, D_MODEL)
    xc2 = ctx.reshape(b * nc, D_MODEL)
    for l in range(nl):
        last = l == nl - 1
        mod_x = mods[l, :b].reshape(b, 1, 6 * D_MODEL)
        mod_c = jnp.broadcast_to(mods[l, b].reshape(1, 1, 6 * D_MODEL), (b, 1, 6 * D_MODEL))

        u_x, hx = _mix_inputs(x2, mod_x, p, g_mix3, l, n)
        u_c, hc = _mix_inputs(xc2, mod_c, p, g_mix3, l, nc)
        k_c, v_c = _keys_values(u_c, p, g_kvl3, rope_c, l, b, nc)
        k_x, v_x = _keys_values(u_x, p, g_kvl3, rope_x, l, b, n)
        q_x = _q_proj(u_x, g_ql3, p["w_uq"], p["g_qn"], rope_x, l, b, n)
        att_x = _attention(q_x, (k_c, v_c, k_x, v_x))
        x2 = _mix_finish(x2, u_x, hx, att_x, mod_x, p, loc, t_pos_x, w_chan, l, b, n)
        if not last:
            q_c = _q_proj(u_c, g_ql3, p["w_uq"], p["g_qn"], rope_c, l, b, nc)
            att_c = _attention(q_c, (k_c, v_c))
            xc2 = _mix_finish(xc2, u_c, hc, att_c, mod_c, p, loc, t_pos_c, w_chan, l, b, nc)

        x2 = _moe(x2, mod_x, p, g_ffn3, l, b, n, 1 if n >= 1024 else b)
        if not last:
            xc2 = _moe(xc2, mod_c, p, g_ffn3, l, b, nc, b)
    return x2.reshape(b, n, D_MODEL)
```

```python
import functools
import math

import jax
import jax.numpy as jnp
from jax import lax
from jax.experimental import pallas as pl
from jax.experimental.pallas import tpu as pltpu

F32 = jnp.float32
BF16 = jnp.bfloat16

D_MODEL = 2048
DEPTH = 4
GRID_W = 64
N_BRANCH = 4
BRANCH_W = 512
MLA_HEADS = 8
V_HEAD = 64
QK_NOPE = 64
QK_ROPE = 32
QK_HEAD = 96
ROPE_PAIRS = 8
Q_LORA = 512
KV_LORA = 256
ROPE_THETA = 10000.0
CONF_K = 31
SC_K = 3
FNET_GROUPS = 4
FNET_GW = 128
N_EXPERTS = 16
EXPERT_FF = 1024
EC_CAPACITY_FACTOR = 2
EPS = 1e-6
OFF_KR = 768
OFF_CONF = 800

LANES = 128
SUBLANES = 8
HALO = 16
IN_PAD_W = 4096
COL_BLK = 512
VMEM_LIMIT = 56 * 1024 * 1024

Q_SCALE = QK_HEAD ** -0.5 * math.log2(math.e)
_NT = (((1,), (1,)), ((), ()))


def _cparams(sem):
    return pltpu.CompilerParams(dimension_semantics=sem, vmem_limit_bytes=VMEM_LIMIT)


def _dot(a, b):
    return jnp.dot(a, b, preferred_element_type=F32)


def _dot_nt(a, b):
    return lax.dot_general(a, b, _NT, preferred_element_type=F32)


def _split2(a):
    hi = a.astype(BF16)
    lo = (a - hi.astype(F32)).astype(BF16)
    return hi, lo


def _dot3(a, b, nt=False):
    f = _dot_nt if nt else _dot
    ah, al = _split2(a)
    bh, bl = _split2(b)
    return f(ah, bh) + (f(al, bh) + f(ah, bl))


def _rms(x, g, width):
    ms = jnp.sum(x * x, axis=-1, keepdims=True) * (1.0 / width)
    return x * lax.rsqrt(ms + EPS) * g


def _silu(x):
    return x * jax.nn.sigmoid(x)


def _mod_kernel(s_ref, w_ref, b_ref, o_ref):
    s = _silu(s_ref[...])
    o_ref[...] = _dot3(s, w_ref[...]) + b_ref[...]


def _mod_all(cond, w_mod, b_mod):
    nl = w_mod.shape[0]
    tn = 1024
    return pl.pallas_call(
        _mod_kernel,
        out_shape=jax.ShapeDtypeStruct((nl, 8, 6 * D_MODEL), F32),
        grid=(nl, 6 * D_MODEL // tn),
        in_specs=[
            pl.BlockSpec((8, D_MODEL), lambda l, j: (0, 0)),
            pl.BlockSpec((None, D_MODEL, tn), lambda l, j: (l, 0, j)),
            pl.BlockSpec((None, 1, tn), lambda l, j: (l, 0, j)),
        ],
        out_specs=pl.BlockSpec((None, 8, tn), lambda l, j: (l, 0, j)),
        compiler_params=_cparams(("parallel", "parallel")),
        name="adaln_mod",
    )(cond, w_mod, b_mod.reshape(nl, 1, 6 * D_MODEL))


def _in_proj_kernel(x_ref, g_ref, sh_ref, sc_ref, w_ref, u_ref, hx_ref, *, chunk):
    @pl.when(pl.program_id(1) == 0)
    def _():
        g, sc, sh = g_ref[...], 1.0 + sc_ref[0], sh_ref[0]
        for r0 in range(0, x_ref.shape[0], chunk):
            y = _rms(x_ref[r0:r0 + chunk, :], g, D_MODEL)
            hx_ref[r0:r0 + chunk, :] = (y * sc + sh).astype(BF16)

    u_ref[...] = _dot(hx_ref[...], w_ref[...]).astype(BF16)


def _in_proj(x2, g_all, mod3, w_all, l, n):
    m = x2.shape[0]
    tm = min(1024, n)
    tpb = n // tm
    tn = 1024
    return pl.pallas_call(
        functools.partial(_in_proj_kernel, chunk=256),
        out_shape=(jax.ShapeDtypeStruct((m, IN_PAD_W), BF16), jax.ShapeDtypeStruct((m, D_MODEL), BF16)),
        grid=(m // tm, IN_PAD_W // tn),
        in_specs=[
            pl.BlockSpec((tm, D_MODEL), lambda i, j: (i, 0)),
            pl.BlockSpec((None, 1, D_MODEL), lambda i, j: (l, 0, 0)),
            pl.BlockSpec((1, 1, D_MODEL), lambda i, j: (i // tpb, 0, 0)),
            pl.BlockSpec((1, 1, D_MODEL), lambda i, j: (i // tpb, 0, 1)),
            pl.BlockSpec((None, D_MODEL, tn), lambda i, j: (l, 0, j)),
        ],
        out_specs=(
            pl.BlockSpec((tm, tn), lambda i, j: (i, j)),
            pl.BlockSpec((tm, D_MODEL), lambda i, j: (i, 0)),
        ),
        compiler_params=_cparams(("parallel", "arbitrary")),
        name="norm_in_proj",
    )(x2, g_all, mod3, mod3, w_all)


def _head_norm_rope(xh, gn, c, s1, s2):
    xh = _rms(xh, gn, QK_HEAD)
    return xh * c + pltpu.roll(xh, LANES - ROPE_PAIRS, 1) * s1 + pltpu.roll(xh, ROPE_PAIRS, 1) * s2


def _q_kernel(cq_ref, gl_ref, w_ref, gn_ref, c_ref, s1_ref, s2_ref, q_ref):
    y = _rms(cq_ref[...].astype(F32), gl_ref[...], Q_LORA).astype(BF16)
    q = _dot(y, w_ref[...])
    gn = gn_ref[...]
    c, s1, s2 = c_ref[...], s1_ref[...], s2_ref[...]
    for h in range(MLA_HEADS):
        qh = _head_norm_rope(q[:, h * LANES:(h + 1) * LANES], gn, c, s1, s2)
        q_ref[0, h] = (qh * Q_SCALE).astype(BF16)


def _q_proj(u, gl_all, w_all, gn_all, rope, l, b, n):
    m = u.shape[0]
    tm = min(512, n)
    tpb = n // tm
    tab = pl.BlockSpec((tm, LANES), lambda i: (i % tpb, 0))
    return pl.pallas_call(
        _q_kernel,
        out_shape=jax.ShapeDtypeStruct((b, MLA_HEADS, n, LANES), BF16),
        grid=(m // tm,),
        in_specs=[
            pl.BlockSpec((tm, COL_BLK), lambda i: (i, 0)),
            pl.BlockSpec((None, 1, Q_LORA), lambda i: (l, 0, 0)),
            pl.BlockSpec((None, Q_LORA, MLA_HEADS * LANES), lambda i: (l, 0, 0)),
            pl.BlockSpec((None, 1, LANES), lambda i: (l, 0, 0)),
            tab, tab, tab,
        ],
        out_specs=pl.BlockSpec((1, MLA_HEADS, tm, LANES), lambda i: (i // tpb, 0, i % tpb, 0)),
        compiler_params=_cparams(("parallel",)),
        name="mla_q",
    )(u, gl_all, w_all, gn_all, *rope)


def _kv_kernel(ck_ref, gl_ref, wk_ref, wv_ref, gn_ref, c_ref, s1_ref, s2_ref, k_ref, v_ref):
    blk = ck_ref[...].astype(F32)
    y = _rms(blk[:, :KV_LORA], gl_ref[...], KV_LORA).astype(BF16)
    kn = _dot(y, wk_ref[...])
    v = _dot(y, wv_ref[...])
    kr = blk[:, KV_LORA:KV_LORA + LANES]
    gn = gn_ref[...]
    c, s1, s2 = c_ref[...], s1_ref[...], s2_ref[...]
    for h in range(MLA_HEADS):
        kh = _head_norm_rope(kn[:, h * LANES:(h + 1) * LANES] + kr, gn, c, s1, s2)
        k_ref[0, h] = kh.astype(BF16)
        v_ref[0, h] = v[:, h * LANES:(h + 1) * LANES].astype(BF16)


def _kv_proj(u, gl_all, wk_all, wv_all, gn_all, rope, l, b, n):
    m = u.shape[0]
    tm = min(512, n)
    tpb = n // tm
    tab = pl.BlockSpec((tm, LANES), lambda i: (i % tpb, 0))
    hw = MLA_HEADS * LANES
    o_spec = pl.BlockSpec((1, MLA_HEADS, tm, LANES), lambda i: (i // tpb, 0, i % tpb, 0))
    o_shape = jax.ShapeDtypeStruct((b, MLA_HEADS, n, LANES), BF16)
    return pl.pallas_call(
        _kv_kernel,
        out_shape=(o_shape, o_shape),
        grid=(m // tm,),
        in_specs=[
            pl.BlockSpec((tm, COL_BLK), lambda i: (i, 1)),
            pl.BlockSpec((None, 1, KV_LORA), lambda i: (l, 0, 0)),
            pl.BlockSpec((None, KV_LORA, hw), lambda i: (l, 0, 0)),
            pl.BlockSpec((None, KV_LORA, hw), lambda i: (l, 0, 0)),
            pl.BlockSpec((None, 1, LANES), lambda i: (l, 0, 0)),
            tab, tab, tab,
        ],
        out_specs=(o_spec, o_spec),
        compiler_params=_cparams(("parallel",)),
        name="mla_kv",
    )(u, gl_all, wk_all, wv_all, gn_all, *rope)


def _attn_kernel(*refs, n_src, key_chunk, q_rows):
    q_ref = refs[0]
    kv = refs[1:1 + 2 * n_src]
    o_ref = refs[1 + 2 * n_src]
    scr = refs[2 + 2 * n_src:]
    chunks = []
    col = 0
    for si in range(n_src):
        nk = kv[2 * si].shape[2]
        for st in range(0, nk, key_chunk):
            sz = min(key_chunk, nk - st)
            chunks.append((si, st, sz, col))
            col += sz
    units = [(hh, r0) for r0 in range(0, q_ref.shape[2], q_rows) for hh in range(2)]

    def scores(u):
        hh, r0 = units[u]
        q = q_ref[0, hh, r0:r0 + q_rows, :]
        mx = None
        for si, st, sz, co in chunks:
            s = _dot_nt(q, kv[2 * si][0, hh, st:st + sz, :])
            scr[u % len(scr)][:, co:co + sz] = s
            cm = jnp.max(s, axis=-1, keepdims=True)
            mx = cm if mx is None else jnp.maximum(mx, cm)
        return mx

    def weighted(u, mx):
        hh, _ = units[u]
        den = None
        acc = None
        for si, st, sz, co in chunks:
            p = jnp.exp2(scr[u % len(scr)][:, co:co + sz] - mx)
            ps = jnp.sum(p, axis=-1, keepdims=True)
            den = ps if den is None else den + ps
            pv = _dot(p.astype(BF16), kv[2 * si + 1][0, hh, st:st + sz, :])
            acc = pv if acc is None else acc + pv
        return acc / den

    outs = []
    mx_prev = scores(0)
    for u in range(1, len(units)):
        mx = scores(u)
        outs.append(weighted(u - 1, mx_prev))
        mx_prev = mx
    outs.append(weighted(len(units) - 1, mx_prev))
    for i in range(len(units) // 2):
        o_ref[0, i * q_rows:(i + 1) * q_rows, :] = (outs[2 * i] + outs[2 * i + 1]).astype(BF16)


def _attention(q, kvs):
    b, _, nq, _ = q.shape
    tq = min(512, nq)
    q_rows = min(256, tq)
    n_src = len(kvs) // 2
    nk_tot = sum(kvs[2 * i].shape[2] for i in range(n_src))
    n_scr = min(3, 2 * tq // q_rows)
    in_specs = [pl.BlockSpec((1, 2, tq, LANES), lambda bb, hp, i: (bb, hp, i, 0))]
    for a in kvs:
        in_specs.append(pl.BlockSpec((1, 2, a.shape[2], LANES), lambda bb, hp, i: (bb, hp, 0, 0)))
    return pl.pallas_call(
        functools.partial(_attn_kernel, n_src=n_src, key_chunk=1024, q_rows=q_rows),
        out_shape=jax.ShapeDtypeStruct((b, nq, BRANCH_W), BF16),
        grid=(b, MLA_HEADS // 2, nq // tq),
        in_specs=in_specs,
        out_specs=pl.BlockSpec((1, tq, LANES), lambda bb, hp, i: (bb, i, hp)),
        scratch_shapes=[pltpu.VMEM((q_rows, nk_tot), F32) for _ in range(n_scr)],
        compiler_params=_cparams(("parallel", "parallel", "parallel")),
        name="mla_attention",
    )(q, *kvs)


def _local_kernel(a_ref, g_ref, sb_ref, scc_ref, sx_ref,
                  ap_ref, gp_ref, cp_ref, xp_ref, an_ref, gn_ref, cn_ref, xn_ref,
                  cw_ref, lg_ref, lb_ref, sw_ref, y_ref, z_ref, g_scr, h_scr, *, tpb, rows):
    i = pl.program_id(0)
    tm = a_ref.shape[0]
    has_prev = (i % tpb) > 0
    has_next = (i % tpb) < tpb - 1
    zero = jnp.zeros((HALO, COL_BLK), F32)
    f = lambda ref: ref[...].astype(F32)

    def fill(scr, slot, sh, prev, curr, nxt):
        scr[slot, 0:HALO - sh, :] = prev[sh:, :]
        scr[slot, HALO - sh:HALO - sh + tm, :] = curr
        scr[slot, HALO - sh + tm:2 * HALO - sh + tm, :] = nxt

    gp = jnp.where(has_prev, f(ap_ref) * jax.nn.sigmoid(f(gp_ref)), zero)
    gc = f(a_ref) * jax.nn.sigmoid(f(g_ref))
    gn = jnp.where(has_next, f(an_ref) * jax.nn.sigmoid(f(gn_ref)), zero)
    for sh in range(SUBLANES):
        fill(g_scr, sh, sh, gp, gc, gn)
    sc_offs = [HALO - SC_K // 2 + j for j in range(SC_K)]
    hp = jnp.where(has_prev, f(cp_ref) * f(xp_ref), zero)
    hc = f(scc_ref) * f(sx_ref)
    hn = jnp.where(has_next, f(cn_ref) * f(xn_ref), zero)
    for j, o in enumerate(sc_offs):
        fill(h_scr, j, o % SUBLANES, hp, hc, hn)
    lg, lb = lg_ref[...], lb_ref[...]
    for c in range(tm // rows):
        r0 = c * rows
        acc = None
        for j in range(CONF_K):
            o = HALO - CONF_K // 2 + j
            st = r0 + o - o % SUBLANES
            t = cw_ref[j:j + 1, :] * g_scr[o % SUBLANES, st:st + rows, :]
            acc = t if acc is None else acc + t
        mu = jnp.mean(acc, axis=-1, keepdims=True)
        dv = acc - mu
        var = jnp.mean(dv * dv, axis=-1, keepdims=True)
        y = dv * lax.rsqrt(var + EPS) * lg + lb
        y_ref[r0:r0 + rows, :] = _silu(y).astype(BF16)
        acc = None
        for j, o in enumerate(sc_offs):
            st = r0 + o - o % SUBLANES
            t = sw_ref[j:j + 1, :] * h_scr[j, st:st + rows, :]
            acc = t if acc is None else acc + t
        z_ref[r0:r0 + rows, :] = (sb_ref[r0:r0 + rows, :].astype(F32) * acc).astype(BF16)


def _local_branches(u, cw_all, lg_all, lb_all, sw_all, l, n):
    m = u.shape[0]
    tm = min(512, n)
    tpb = n // tm
    hb = tm // HALO
    last = m // HALO - 1

    def cur(cb):
        return pl.BlockSpec((tm, COL_BLK), lambda i: (i, cb))

    def prev(cb):
        return pl.BlockSpec((HALO, COL_BLK), lambda i: (jnp.maximum(i * hb - 1, 0), cb))

    def nxt(cb):
        return pl.BlockSpec((HALO, COL_BLK), lambda i: (jnp.minimum((i + 1) * hb, last), cb))

    o_shape = jax.ShapeDtypeStruct((m, COL_BLK), BF16)
    o_spec = pl.BlockSpec((tm, COL_BLK), lambda i: (i, 0))
    return pl.pallas_call(
        functools.partial(_local_kernel, tpb=tpb, rows=32),
        out_shape=(o_shape, o_shape),
        grid=(m // tm,),
        in_specs=[
            cur(2), cur(3), cur(4), cur(5), cur(6),
            prev(2), prev(3), prev(5), prev(6),
            nxt(2), nxt(3), nxt(5), nxt(6),
            pl.BlockSpec((None, CONF_K, COL_BLK), lambda i: (l, 0, 0)),
            pl.BlockSpec((None, 1, COL_BLK), lambda i: (l, 0, 0)),
            pl.BlockSpec((None, 1, COL_BLK), lambda i: (l, 0, 0)),
            pl.BlockSpec((None, SC_K, COL_BLK), lambda i: (l, 0, 0)),
        ],
        out_specs=(o_spec, o_spec),
        scratch_shapes=[pltpu.VMEM((SUBLANES, tm + 2 * HALO, COL_BLK), F32),
                        pltpu.VMEM((SC_K, tm + 2 * HALO, COL_BLK), F32)],
        compiler_params=_cparams(("parallel",)),
        name="local_convs",
    )(u, u, u, u, u, u, u, u, u, u, u, u, u, cw_all, lg_all, lb_all, sw_all)


def _chan_dft_kernel(f_ref, w_ref, o_ref):
    r = _dot(f_ref[...], w_ref[...])
    o_ref[0, 0] = r[:, :COL_BLK].astype(BF16)
    o_ref[0, 1] = r[:, COL_BLK:].astype(BF16)


def _chan_dft(u, w_chan, b, n):
    m = u.shape[0]
    tm = min(512, n)
    tpb = n // tm
    return pl.pallas_call(
        _chan_dft_kernel,
        out_shape=jax.ShapeDtypeStruct((b, 2, n, COL_BLK), BF16),
        grid=(m // tm,),
        in_specs=[
            pl.BlockSpec((tm, COL_BLK), lambda i: (i, 7)),
            pl.BlockSpec((COL_BLK, 2 * COL_BLK), lambda i: (0, 0)),
        ],
        out_specs=pl.BlockSpec((1, 2, tm, COL_BLK), lambda i: (i // tpb, 0, i % tpb, 0)),
        compiler_params=_cparams(("parallel",)),
        name="fnet_chan_dft",
    )(u, w_chan)


def _pos_dft_kernel(t_ref, ab_ref, o_ref, acc_ref):
    k = pl.program_id(2)

    @pl.when(k == 0)
    def _():
        acc_ref[...] = jnp.zeros_like(acc_ref)

    acc_ref[...] += _dot(t_ref[...], ab_ref[0])

    @pl.when(k == pl.num_programs(2) - 1)
    def _():
        o_ref[0] = acc_ref[...].astype(BF16)


def _pos_dft(t_pos, ab):
    b, n2, _ = ab.shape
    n = n2 // 2
    tm = min(1024, n)
    tk = min(2048, n2)
    return pl.pallas_call(
        _pos_dft_kernel,
        out_shape=jax.ShapeDtypeStruct((b, n, COL_BLK), BF16),
        grid=(b, n // tm, n2 // tk),
        in_specs=[
            pl.BlockSpec((tm, tk), lambda bb, i, k: (i, k)),
            pl.BlockSpec((1, tk, COL_BLK), lambda bb, i, k: (bb, k, 0)),
        ],
        out_specs=pl.BlockSpec((1, tm, COL_BLK), lambda bb, i, k: (bb, i, 0)),
        scratch_shapes=[pltpu.VMEM((tm, COL_BLK), F32)],
        compiler_params=_cparams(("parallel", "parallel", "arbitrary")),
        name="fnet_pos_dft",
    )(t_pos, ab)


def _merge_kernel(hx_ref, y0_ref, y1_ref, y2_ref, y3_ref, wg0_ref, wg1_ref, wg2_ref, wg3_ref, wb_ref, o_ref):
    hx = hx_ref[...]
    ys = (y0_ref, y1_ref, y2_ref, y3_ref)
    wgs = (wg0_ref, wg1_ref, wg2_ref, wg3_ref)
    acc = None
    for k in range(N_BRANCH):
        t = jax.nn.sigmoid(_dot(hx, wgs[k][...])) * _dot(ys[k][...], wb_ref[k])
        acc = t if acc is None else acc + t
    o_ref[...] = acc.astype(BF16)


def _merge(hx, ys, wg_all, wb_all, l):
    m = hx.shape[0]
    tm = min(1024, m)
    tn = 512
    nj = D_MODEL // tn
    y_spec = pl.BlockSpec((tm, BRANCH_W), lambda i, j: (i, 0))
    wg_specs = [pl.BlockSpec((None, D_MODEL, tn), lambda i, j, k=k: (l, 0, k * nj + j)) for k in range(N_BRANCH)]
    return pl.pallas_call(
        _merge_kernel,
        out_shape=jax.ShapeDtypeStruct((m, D_MODEL), BF16),
        grid=(m // tm, nj),
        in_specs=[pl.BlockSpec((tm, D_MODEL), lambda i, j: (i, 0)), y_spec, y_spec, y_spec, y_spec,
                  *wg_specs,
                  pl.BlockSpec((None, N_BRANCH, BRANCH_W, tn), lambda i, j: (l, 0, 0, j))],
        out_specs=pl.BlockSpec((tm, tn), lambda i, j: (i, j)),
        compiler_params=_cparams(("parallel", "parallel")),
        name="branch_merge",
    )(hx, *ys, wg_all, wg_all, wg_all, wg_all, wb_all)


def _out_proj_kernel(m_ref, w_ref, x_ref, g_ref, o_ref):
    o_ref[...] = x_ref[...] + g_ref[0] * _dot(m_ref[...], w_ref[...])


def _out_proj(mm, w_all, x2, mod3, l, n):
    m = mm.shape[0]
    tm = min(1024, n)
    tpb = n // tm
    tn = 512
    nj = D_MODEL // tn
    return pl.pallas_call(
        _out_proj_kernel,
        out_shape=jax.ShapeDtypeStruct((m, D_MODEL), F32),
        grid=(m // tm, nj),
        in_specs=[
            pl.BlockSpec((tm, D_MODEL), lambda i, j: (i, 0)),
            pl.BlockSpec((None, D_MODEL, tn), lambda i, j: (l, 0, j)),
            pl.BlockSpec((tm, tn), lambda i, j: (i, j)),
            pl.BlockSpec((1, 1, tn), lambda i, j: (i // tpb, 0, 2 * nj + j)),
        ],
        out_specs=pl.BlockSpec((tm, tn), lambda i, j: (i, j)),
        compiler_params=_cparams(("parallel", "parallel")),
        name="out_proj_residual",
    )(mm, w_all, x2, mod3)


def _ffn_norm(x, g, sh, sc):
    return _rms(x, g, D_MODEL) * (1.0 + sc) + sh


def _router_kernel(x_ref, g_ref, sh_ref, sc_ref, wr_ref, aff_ref):
    hx = _ffn_norm(x_ref[...], g_ref[...], sh_ref[0], sc_ref[0])
    lt = _dot3(wr_ref[...], hx, nt=True)
    e = jnp.exp(lt - jnp.max(lt, axis=0, keepdims=True))
    aff_ref[0] = e / jnp.sum(e, axis=0, keepdims=True)


def _router(x2, g_all, mod3, wr_t_all, l, b, n):
    m = x2.shape[0]
    tm = min(512, n)
    tpb = n // tm
    return pl.pallas_call(
        _router_kernel,
        out_shape=jax.ShapeDtypeStruct((b, N_EXPERTS, n), F32),
        grid=(m // tm,),
        in_specs=[
            pl.BlockSpec((tm, D_MODEL), lambda i: (i, 0)),
            pl.BlockSpec((None, 1, D_MODEL), lambda i: (l, 0, 0)),
            pl.BlockSpec((1, 1, D_MODEL), lambda i: (i // tpb, 0, 3)),
            pl.BlockSpec((1, 1, D_MODEL), lambda i: (i // tpb, 0, 4)),
            pl.BlockSpec((None, N_EXPERTS, D_MODEL), lambda i: (l, 0, 0)),
        ],
        out_specs=pl.BlockSpec((1, N_EXPERTS, tm), lambda i: (i // tpb, 0, i % tpb)),
        compiler_params=_cparams(("parallel",)),
        name="norm_router",
    )(x2, g_all, mod3, mod3, wr_t_all)


def _prefix_count(mask01, tri):
    n = mask01.shape[1]
    carry = jnp.zeros((mask01.shape[0], 1), F32)
    outs = []
    for c in range(n // LANES):
        loc = _dot(mask01[:, c * LANES:(c + 1) * LANES], tri) + carry
        outs.append(loc)
        carry = loc[:, LANES - 1:LANES]
    return jnp.concatenate(outs, axis=1)


def _topk_kernel(a_ref, idx_ref, gate_ref, *, cap):
    a = a_ref[0]
    ne, n = a.shape
    bits = pltpu.bitcast(a, jnp.int32)

    def search(i, t):
        cand = t | (jnp.int32(1) << (30 - i))
        c = jnp.sum(jnp.where(bits >= cand, 1.0, 0.0), axis=-1, keepdims=True)
        return jnp.where(c >= cap, cand, t)

    thr = lax.fori_loop(0, 31, search, jnp.zeros((ne, 1), jnp.int32))
    gt = bits > thr
    eq = bits == thr
    need = cap - jnp.sum(jnp.where(gt, 1.0, 0.0), axis=-1, keepdims=True)
    row = lax.broadcasted_iota(jnp.int32, (LANES, LANES), 0)
    colm = lax.broadcasted_iota(jnp.int32, (LANES, LANES), 1)
    tri = jnp.where(row <= colm, 1.0, 0.0).astype(BF16)
    eq_rank = _prefix_count(jnp.where(eq, 1.0, 0.0).astype(BF16), tri)
    sel = gt | (eq & (eq_rank <= need))
    cnt = _prefix_count(jnp.where(sel, 1.0, 0.0).astype(BF16), tri)
    slot = jnp.where(sel, cnt - 1.0, -1.0)

    tok = lax.broadcasted_iota(jnp.int32, (1, n), 1)
    t_hi = (tok >> 6).astype(F32)
    t_lo = (tok & 63).astype(F32)
    r_iota = lax.broadcasted_iota(jnp.int32, (cap, 1), 0).astype(F32)
    zeros3 = jnp.zeros((3, n), F32)
    for e in range(ne):
        onehot = jnp.where(slot[e:e + 1, :] == r_iota, 1.0, 0.0).astype(BF16)
        ae = a[e:e + 1, :]
        a1 = ae.astype(BF16).astype(F32)
        a2 = (ae - a1).astype(BF16).astype(F32)
        a3 = ae - a1 - a2
        rows = jnp.concatenate([t_hi, t_lo, a1, a2, a3, zeros3], axis=0).astype(BF16)
        res = _dot_nt(rows, onehot)
        idx_ref[0, e:e + 1, :] = (res[0:1] * 64.0 + res[1:2]).astype(jnp.int32)
        gate_ref[0, e:e + 1, :] = res[2:3] + res[3:4] + res[4:5]


def _topk(aff_t, cap):
    b, ne, n = aff_t.shape
    return pl.pallas_call(
        functools.partial(_topk_kernel, cap=cap),
        out_shape=(jax.ShapeDtypeStruct((b, ne, cap), jnp.int32),
                   jax.ShapeDtypeStruct((b, ne, cap), F32)),
        grid=(b,),
        in_specs=[pl.BlockSpec((1, ne, n), lambda bb: (bb, 0, 0))],
        out_specs=(pl.BlockSpec((1, ne, cap), lambda bb: (bb, 0, 0)),
                   pl.BlockSpec((1, ne, cap), lambda bb: (bb, 0, 0))),
        compiler_params=_cparams(("parallel",)),
        name="expert_topk",
    )(aff_t)


def _swiglu_rows(xh, w1_ref, w3_ref, w2_ref, n_chunks, before_chunk=None):
    fc = EXPERT_FF // n_chunks
    y = None
    for c in range(n_chunks):
        anchor = before_chunk(c) if before_chunk is not None else None
        cs = slice(c * fc, (c + 1) * fc)
        h = (_silu(_dot(xh, w1_ref[:, cs])) * _dot(xh, w3_ref[:, cs])).astype(BF16)
        if anchor is not None:
            h = h + anchor
        t = _dot(h, w2_ref[cs, :])
        y = t if y is None else y + t
    return y


def _expert_seq_kernel(idx_ref, xin_hbm, gf_ref, sh_ref, sc_ref, w1_ref, w3_ref, w2_ref, gate_ref, g2_ref,
                       xo_hbm, xb, xh, sem_g, sem_w, *, n, cap, bg):
    del xin_hbm
    e = pl.program_id(0)
    g = pl.program_id(1)
    rows = bg * cap

    def for_rows(fn):
        for bi in range(bg):
            b_ = g * bg + bi
            base = (b_ * N_EXPERTS + e) * cap

            def body(r, carry, b_=b_, base=base, bi=bi):
                fn(b_ * n + idx_ref[base + r], bi * cap + r)
                return carry

            lax.fori_loop(0, cap, body, 0)

    for_rows(lambda src, dst: pltpu.make_async_copy(
        xo_hbm.at[pl.ds(src, 1)], xb.at[pl.ds(dst, 1)], sem_g).start())
    pltpu.make_async_copy(xo_hbm.at[pl.ds(0, rows)], xb, sem_g).wait()
    gf = gf_ref[...]
    for bi in range(bg):
        sl = slice(bi * cap, (bi + 1) * cap)
        xh[sl, :] = _ffn_norm(xb[sl, :], gf, sh_ref[bi], sc_ref[bi]).astype(BF16)
    y = _swiglu_rows(xh[...], w1_ref, w3_ref, w2_ref, 1)
    for bi in range(bg):
        sl = slice(bi * cap, (bi + 1) * cap)
        xb[sl, :] = xb[sl, :] + y[sl, :] * gate_ref[bi, 0] * g2_ref[bi]
    for_rows(lambda src, dst: pltpu.make_async_copy(
        xb.at[pl.ds(dst, 1)], xo_hbm.at[pl.ds(src, 1)], sem_w).start())
    pltpu.make_async_copy(xb, xo_hbm.at[pl.ds(0, rows)], sem_w).wait()


def _expert_pipe_kernel(idx_ref, xin_hbm, gf_ref, sh_ref, sc_ref, w1_ref, w3_ref, w2_ref, gate_ref, g2_ref,
                        xo_hbm, xb, xh, sem_g, sem_w, *, n, cap, nbg, n_chunks):
    del xin_hbm
    e = pl.program_id(0)
    g = pl.program_id(1)
    s = e * nbg + g
    total = N_EXPERTS * nbg
    slot = s % 3
    nxt_slot = (s + 1) % 3
    prv_slot = (s + 2) % 3

    def rows_of(t):
        e_, g_ = t // nbg, t % nbg
        return (g_ * N_EXPERTS + e_) * cap, g_ * n

    def gather(base, row0, slot_, r):
        return pltpu.make_async_copy(xo_hbm.at[pl.ds(row0 + idx_ref[base + r], 1)],
                                     xb.at[slot_, pl.ds(r, 1)], sem_g.at[slot_])

    def scatter(base, row0, slot_, r):
        return pltpu.make_async_copy(xb.at[slot_, pl.ds(r, 1)],
                                     xo_hbm.at[pl.ds(row0 + idx_ref[base + r], 1)], sem_w)

    def loop_rows(fn):
        def body(r, carry):
            fn(r)
            return carry
        lax.fori_loop(0, cap, body, 0)

    def wait_gather(slot_):
        pltpu.make_async_copy(xo_hbm.at[pl.ds(0, cap)], xb.at[slot_], sem_g.at[slot_]).wait()

    def wait_scatter():
        pltpu.make_async_copy(xb.at[0], xo_hbm.at[pl.ds(0, cap)], sem_w).wait()

    cur = rows_of(s)
    prv = rows_of(jnp.where(s == 0, nbg - 1, s - 1))
    nxt = rows_of(jnp.minimum(s + 1, total - 1))

    @pl.when(s == 0)
    def _():
        loop_rows(lambda r: gather(*cur, slot, r).start())
        loop_rows(lambda r: gather(*prv, prv_slot, r).start())
        wait_gather(prv_slot)

    wait_gather(slot)

    @pl.when(s > 0)
    def _():
        wait_scatter()

    xh[...] = _ffn_norm(xb[slot], gf_ref[...], sh_ref[0], sc_ref[0]).astype(BF16)
    per = cap // n_chunks

    def row_traffic(c):
        for r in range(c * per, (c + 1) * per):
            gather(*nxt, nxt_slot, r).start()
            scatter(*prv, prv_slot, r).start()
        return (xb[slot, 0:1, 0:1] * 0.0).astype(BF16)

    y = _swiglu_rows(xh[...], w1_ref, w3_ref, w2_ref, n_chunks, row_traffic)
    xb[slot] = xb[slot] + y * gate_ref[0, 0] * g2_ref[0]

    @pl.when(s == total - 1)
    def _():
        wait_gather(nxt_slot)
        wait_scatter()
        loop_rows(lambda r: scatter(*cur, slot, r).start())
        wait_scatter()


def _experts(idx_flat, x2, gf_all, w1_all, w3_all, w2_all, gate_col, mod3, l, b, n, cap, bg):
    nbg = b // bg
    rows = bg * cap
    pipelined = bg == 1 and nbg >= 3
    if pipelined:
        body = functools.partial(_expert_pipe_kernel, n=n, cap=cap, nbg=nbg, n_chunks=4)
        scratch = [pltpu.VMEM((3, rows, D_MODEL), F32), pltpu.VMEM((rows, D_MODEL), BF16),
                   pltpu.SemaphoreType.DMA((3,)), pltpu.SemaphoreType.DMA]
    else:
        body = functools.partial(_expert_seq_kernel, n=n, cap=cap, bg=bg)
        scratch = [pltpu.VMEM((rows, D_MODEL), F32), pltpu.VMEM((rows, D_MODEL), BF16),
                   pltpu.SemaphoreType.DMA, pltpu.SemaphoreType.DMA]
    return pl.pallas_call(
        body,
        out_shape=jax.ShapeDtypeStruct(x2.shape, F32),
        grid_spec=pltpu.PrefetchScalarGridSpec(
            num_scalar_prefetch=1,
            grid=(N_EXPERTS, nbg),
            in_specs=[
                pl.BlockSpec(memory_space=pl.ANY),
                pl.BlockSpec((None, 1, D_MODEL), lambda e, g, idx: (l, 0, 0)),
                pl.BlockSpec((bg, 1, D_MODEL), lambda e, g, idx: (g, 0, 3)),
                pl.BlockSpec((bg, 1, D_MODEL), lambda e, g, idx: (g, 0, 4)),
                pl.BlockSpec((None, None, D_MODEL, EXPERT_FF), lambda e, g, idx: (l, e, 0, 0)),
                pl.BlockSpec((None, None, D_MODEL, EXPERT_FF), lambda e, g, idx: (l, e, 0, 0)),
                pl.BlockSpec((None, None, EXPERT_FF, D_MODEL), lambda e, g, idx: (l, e, 0, 0)),
                pl.BlockSpec((bg, 1, cap, 1), lambda e, g, idx: (g, e, 0, 0)),
                pl.BlockSpec((bg, 1, D_MODEL), lambda e, g, idx: (g, 0, 5)),
            ],
            out_specs=pl.BlockSpec(memory_space=pl.ANY),
            scratch_shapes=scratch,
        ),
        input_output_aliases={1: 0},
        compiler_params=_cparams(("arbitrary", "arbitrary")),
        name="expert_ffn",
    )(idx_flat, x2, gf_all, mod3, mod3, w1_all, w3_all, w2_all, gate_col, mod3)


def _rope_tables(n, use_rope):
    ones_nope = jnp.ones((n, QK_NOPE), F32)
    pad = jnp.zeros((n, LANES - QK_HEAD), F32)
    if not use_rope:
        z = jnp.zeros((n, LANES), F32)
        return jnp.concatenate([ones_nope, jnp.ones((n, QK_ROPE), F32), pad], axis=1), z, z
    rows = n // GRID_W
    t_row = jnp.repeat(jnp.arange(rows, dtype=F32), GRID_W)
    t_col = jnp.tile(jnp.arange(GRID_W, dtype=F32), rows)
    inv = ROPE_THETA ** (-jnp.arange(ROPE_PAIRS, dtype=F32) / ROPE_PAIRS)
    ang = jnp.stack([t_row[:, None] * inv, t_col[:, None] * inv], axis=1)
    cos, sin = jnp.cos(ang), jnp.sin(ang)
    z8 = jnp.zeros_like(sin)
    c = jnp.stack([cos, cos], axis=2).reshape(n, QK_ROPE)
    s1 = jnp.stack([-sin, z8], axis=2).reshape(n, QK_ROPE)
    s2 = jnp.stack([z8, sin], axis=2).reshape(n, QK_ROPE)
    zn = jnp.zeros((n, QK_NOPE), F32)
    return (jnp.concatenate([ones_nope, c, pad], axis=1),
            jnp.concatenate([zn, s1, pad], axis=1),
            jnp.concatenate([zn, s2, pad], axis=1))


def _dft_tables(size, split):
    j = jnp.arange(size, dtype=jnp.int32)[:, None]
    na = size // split
    a = jnp.arange(na, dtype=jnp.int32)[None, :]
    bb = jnp.arange(split, dtype=jnp.int32)[None, :]
    ang_a = (2.0 * math.pi / na) * ((j * a) % na).astype(F32)
    ang_b = (2.0 * math.pi / size) * ((j * bb) % size).astype(F32)
    ca, sa, cb, sb = jnp.cos(ang_a), jnp.sin(ang_a), jnp.cos(ang_b), jnp.sin(ang_b)
    cos = (ca[:, :, None] * cb[:, None, :] - sa[:, :, None] * sb[:, None, :]).reshape(size, size)
    sin = (sa[:, :, None] * cb[:, None, :] + ca[:, :, None] * sb[:, None, :]).reshape(size, size)
    return cos, sin


def _pos_dft_table(n):
    split = 1
    while split * split < n:
        split *= 2
    cos, sin = _dft_tables(n, split)
    return (jnp.concatenate([cos, sin], axis=1) * (n ** -0.5)).astype(BF16)


def _chan_dft_table():
    cos, sin = _dft_tables(FNET_GW, 16)
    eye = jnp.eye(FNET_GROUPS, dtype=F32)
    scale = FNET_GW ** -0.5
    return jnp.concatenate([jnp.kron(eye, cos * scale), jnp.kron(eye, -sin * scale)], axis=1).astype(BF16)


def _pad_heads(w, width):
    nl, k, _ = w.shape
    w = w.reshape(nl, k, MLA_HEADS, width)
    return jnp.pad(w, ((0, 0), (0, 0), (0, 0), (0, LANES - width))).reshape(nl, k, MLA_HEADS * LANES)


def _prep_weights(w_in, w_uq, w_uk, w_uv, g_qn, g_kn, w_gate, w_branch, w_out, w_router, w_e1, w_e3, w_e2):
    nl = w_in.shape[0]
    z = lambda w: jnp.zeros((nl, D_MODEL, w), F32)
    w_in_p = jnp.concatenate([
        w_in[:, :, :OFF_KR], z(QK_NOPE), w_in[:, :, OFF_KR:OFF_CONF], z(2 * COL_BLK - OFF_KR - QK_HEAD),
        w_in[:, :, OFF_CONF:]], axis=2).astype(BF16)
    w_uv_h = w_uv.reshape(nl, KV_LORA, MLA_HEADS // 2, 2, V_HEAD)
    zv = jnp.zeros_like(w_uv_h[:, :, :, 0])
    w_uv_p = jnp.stack([jnp.concatenate([w_uv_h[:, :, :, 0], zv], axis=-1),
                        jnp.concatenate([zv, w_uv_h[:, :, :, 1]], axis=-1)], axis=3)
    padg = lambda g: jnp.pad(g, ((0, 0), (0, LANES - QK_HEAD))).reshape(nl, 1, LANES)
    return dict(
        w_in=w_in_p,
        w_uq=_pad_heads(w_uq, QK_HEAD).astype(BF16),
        w_uk=_pad_heads(w_uk, QK_NOPE).astype(BF16),
        w_uv=w_uv_p.reshape(nl, KV_LORA, MLA_HEADS * LANES).astype(BF16),
        g_qn=padg(g_qn), g_kn=padg(g_kn),
        w_gate=w_gate.astype(BF16), w_branch=w_branch.astype(BF16), w_out=w_out.astype(BF16),
        w_router_t=jnp.swapaxes(w_router, 1, 2),
        w_e1=w_e1.astype(BF16), w_e3=w_e3.astype(BF16), w_e2=w_e2.astype(BF16),
    )


def _mix_inputs(x2, mod3, p, g_mix, l, n):
    return _in_proj(x2, g_mix, mod3, p["w_in"], l, n)


def _keys_values(u, p, g_kv_lora, rope, l, b, n):
    return _kv_proj(u, g_kv_lora, p["w_uk"], p["w_uv"], p["g_kn"], rope, l, b, n)


def _mix_finish(x2, u, hx, att, mod3, p, loc, t_pos, w_chan, l, b, n):
    m = x2.shape[0]
    y_conf, y_sc = _local_branches(u, loc["conf_dw"], loc["conf_ln_g"], loc["conf_ln_b"], loc["sc_dw"], l, n)
    ab = _chan_dft(u, w_chan, b, n)
    y_f = _pos_dft(t_pos, ab.reshape(b, 2 * n, COL_BLK)).reshape(m, COL_BLK)
    mm = _merge(hx, (att.reshape(m, BRANCH_W), y_conf, y_sc, y_f), p["w_gate"], p["w_branch"], l)
    return _out_proj(mm, p["w_out"], x2, mod3, l, n)


def _moe(x2, mod3, p, g_ffn, l, b, n, bg):
    cap = EC_CAPACITY_FACTOR * n // N_EXPERTS
    aff_t = _router(x2, g_ffn, mod3, p["w_router_t"], l, b, n)
    idx, gate = _topk(aff_t, cap)
    return _experts(idx.reshape(-1), x2, g_ffn, p["w_e1"], p["w_e3"], p["w_e2"],
                    gate.reshape(b, N_EXPERTS, cap, 1), mod3, l, b, n, cap, bg)


def kernel(x, c, ctx, c_ctx, w_mod, b_mod, g_mix, g_ffn, w_in, g_q_lora, w_uq, g_kv_lora, w_uk, w_uv,
           g_qn, g_kn, conf_dw, conf_ln_g, conf_ln_b, sc_dw, w_branch, w_gate, w_out, w_router,
           w_e1, w_e3, w_e2):
    b, n, _ = x.shape
    nc = ctx.shape[1]
    nl = w_mod.shape[0]
    p = _prep_weights(w_in, w_uq, w_uk, w_uv, g_qn, g_kn, w_gate, w_branch, w_out, w_router, w_e1, w_e3, w_e2)
    r3 = lambda g: g.reshape(nl, 1, g.shape[-1])
    g_mix3, g_ffn3, g_ql3, g_kvl3 = r3(g_mix), r3(g_ffn), r3(g_q_lora), r3(g_kv_lora)
    loc = dict(conf_dw=conf_dw, conf_ln_g=r3(conf_ln_g), conf_ln_b=r3(conf_ln_b), sc_dw=sc_dw)
    rope_x = _rope_tables(n, True)
    rope_c = _rope_tables(nc, False)
    t_pos_x = _pos_dft_table(n)
    t_pos_c = _pos_dft_table(nc)
    w_chan = _chan_dft_table()

    cond = jnp.concatenate([c, c_ctx[None, :], jnp.zeros((8 - b - 1, D_MODEL), F32)], axis=0)
    mods = _mod_all(cond, w_mod, b_mod)

    x2 = x.reshape(b * n, D_MODEL)
    xc2 = ctx.reshape(b * nc, D_MODEL)
    for l in range(nl):
        last = l == nl - 1
        mod_x = mods[l, :b].reshape(b, 1, 6 * D_MODEL)
        mod_c = jnp.broadcast_to(mods[l, b].reshape(1, 1, 6 * D_MODEL), (b, 1, 6 * D_MODEL))

        u_x, hx = _mix_inputs(x2, mod_x, p, g_mix3, l, n)
        u_c, hc = _mix_inputs(xc2, mod_c, p, g_mix3, l, nc)
        k_c, v_c = _keys_values(u_c, p, g_kvl3, rope_c, l, b, nc)
        k_x, v_x = _keys_values(u_x, p, g_kvl3, rope_x, l, b, n)
        q_x = _q_proj(u_x, g_ql3, p["w_uq"], p["g_qn"], rope_x, l, b, n)
        att_x = _attention(q_x, (k_c, v_c, k_x, v_x))
        x2 = _mix_finish(x2, u_x, hx, att_x, mod_x, p, loc, t_pos_x, w_chan, l, b, n)
        if not last:
            q_c = _q_proj(u_c, g_ql3, p["w_uq"], p["g_qn"], rope_c, l, b, nc)
            att_c = _attention(q_c, (k_c, v_c))
            xc2 = _mix_finish(xc2, u_c, hc, att_c, mod_c, p, loc, t_pos_c, w_chan, l, b, nc)

        x2 = _moe(x2, mod_x, p, g_ffn3, l, b, n, 1 if n >= 1024 else b)
        if not last:
            xc2 = _moe(xc2, mod_c, p, g_ffn3, l, b, nc, b)
    return x2.reshape(b, n, D_MODEL)
```

```python
import functools
import math

import jax
import jax.numpy as jnp
from jax import lax
from jax.experimental import pallas as pl
from jax.experimental.pallas import tpu as pltpu

F32 = jnp.float32
BF16 = jnp.bfloat16

D_MODEL = 2048
DEPTH = 4
GRID_W = 64
N_BRANCH = 4
BRANCH_W = 512
MLA_HEADS = 8
V_HEAD = 64
QK_NOPE = 64
QK_ROPE = 32
QK_HEAD = 96
ROPE_PAIRS = 8
Q_LORA = 512
KV_LORA = 256
ROPE_THETA = 10000.0
CONF_K = 31
SC_K = 3
FNET_GROUPS = 4
FNET_GW = 128
N_EXPERTS = 16
EXPERT_FF = 1024
EC_CAPACITY_FACTOR = 2
EPS = 1e-6
OFF_KR = 768
OFF_CONF = 800

LANES = 128
SUBLANES = 8
HALO = 16
IN_PAD_W = 4096
COL_BLK = 512
VMEM_LIMIT = 56 * 1024 * 1024

Q_SCALE = QK_HEAD ** -0.5 * math.log2(math.e)
_NT = (((1,), (1,)), ((), ()))


def _cparams(sem):
    return pltpu.CompilerParams(dimension_semantics=sem, vmem_limit_bytes=VMEM_LIMIT)


def _dot(a, b):
    return jnp.dot(a, b, preferred_element_type=F32)


def _dot_nt(a, b):
    return lax.dot_general(a, b, _NT, preferred_element_type=F32)


def _split2(a):
    hi = a.astype(BF16)
    lo = (a - hi.astype(F32)).astype(BF16)
    return hi, lo


def _dot3(a, b, nt=False):
    f = _dot_nt if nt else _dot
    ah, al = _split2(a)
    bh, bl = _split2(b)
    return f(ah, bh) + (f(al, bh) + f(ah, bl))


def _rms(x, g, width):
    ms = jnp.sum(x * x, axis=-1, keepdims=True) * (1.0 / width)
    return x * lax.rsqrt(ms + EPS) * g


def _silu(x):
    return x * jax.nn.sigmoid(x)


def _mod_kernel(s_ref, w_ref, b_ref, o_ref):
    s = _silu(s_ref[...])
    o_ref[...] = _dot3(s, w_ref[...]) + b_ref[...]


def _mod_all(cond, w_mod, b_mod):
    nl = w_mod.shape[0]
    tn = 1024
    return pl.pallas_call(
        _mod_kernel,
        out_shape=jax.ShapeDtypeStruct((nl, 8, 6 * D_MODEL), F32),
        grid=(nl, 6 * D_MODEL // tn),
        in_specs=[
            pl.BlockSpec((8, D_MODEL), lambda l, j: (0, 0)),
            pl.BlockSpec((None, D_MODEL, tn), lambda l, j: (l, 0, j)),
            pl.BlockSpec((None, 1, tn), lambda l, j: (l, 0, j)),
        ],
        out_specs=pl.BlockSpec((None, 8, tn), lambda l, j: (l, 0, j)),
        compiler_params=_cparams(("parallel", "parallel")),
        name="adaln_mod",
    )(cond, w_mod, b_mod.reshape(nl, 1, 6 * D_MODEL))


def _in_proj_kernel(x_ref, g_ref, sh_ref, sc_ref, w_ref, u_ref, hx_ref, *, chunk):
    @pl.when(pl.program_id(1) == 0)
    def _():
        g, sc, sh = g_ref[...], 1.0 + sc_ref[0], sh_ref[0]
        for r0 in range(0, x_ref.shape[0], chunk):
            y = _rms(x_ref[r0:r0 + chunk, :], g, D_MODEL)
            hx_ref[r0:r0 + chunk, :] = (y * sc + sh).astype(BF16)

    u_ref[...] = _dot(hx_ref[...], w_ref[...]).astype(BF16)


def _in_proj(x2, g_all, mod3, w_all, l, n):
    m = x2.shape[0]
    tm = min(1024, n)
    tpb = n // tm
    tn = 1024
    return pl.pallas_call(
        functools.partial(_in_proj_kernel, chunk=256),
        out_shape=(jax.ShapeDtypeStruct((m, IN_PAD_W), BF16), jax.ShapeDtypeStruct((m, D_MODEL), BF16)),
        grid=(m // tm, IN_PAD_W // tn),
        in_specs=[
            pl.BlockSpec((tm, D_MODEL), lambda i, j: (i, 0)),
            pl.BlockSpec((None, 1, D_MODEL), lambda i, j: (l, 0, 0)),
            pl.BlockSpec((1, 1, D_MODEL), lambda i, j: (i // tpb, 0, 0)),
            pl.BlockSpec((1, 1, D_MODEL), lambda i, j: (i // tpb, 0, 1)),
            pl.BlockSpec((None, D_MODEL, tn), lambda i, j: (l, 0, j)),
        ],
        out_specs=(
            pl.BlockSpec((tm, tn), lambda i, j: (i, j)),
            pl.BlockSpec((tm, D_MODEL), lambda i, j: (i, 0)),
        ),
        compiler_params=_cparams(("parallel", "arbitrary")),
        name="norm_in_proj",
    )(x2, g_all, mod3, mod3, w_all)


def _head_norm_rope(xh, gn, c, s1, s2):
    xh = _rms(xh, gn, QK_HEAD)
    return xh * c + pltpu.roll(xh, LANES - ROPE_PAIRS, 1) * s1 + pltpu.roll(xh, ROPE_PAIRS, 1) * s2


def _q_kernel(cq_ref, gl_ref, w_ref, gn_ref, c_ref, s1_ref, s2_ref, q_ref):
    y = _rms(cq_ref[...].astype(F32), gl_ref[...], Q_LORA).astype(BF16)
    q = _dot(y, w_ref[...])
    gn = gn_ref[...]
    c, s1, s2 = c_ref[...], s1_ref[...], s2_ref[...]
    for h in range(MLA_HEADS):
        qh = _head_norm_rope(q[:, h * LANES:(h + 1) * LANES], gn, c, s1, s2)
        q_ref[0, h] = (qh * Q_SCALE).astype(BF16)


def _q_proj(u, gl_all, w_all, gn_all, rope, l, b, n):
    m = u.shape[0]
    tm = min(512, n)
    tpb = n // tm
    tab = pl.BlockSpec((tm, LANES), lambda i: (i % tpb, 0))
    return pl.pallas_call(
        _q_kernel,
        out_shape=jax.ShapeDtypeStruct((b, MLA_HEADS, n, LANES), BF16),
        grid=(m // tm,),
        in_specs=[
            pl.BlockSpec((tm, COL_BLK), lambda i: (i, 0)),
            pl.BlockSpec((None, 1, Q_LORA), lambda i: (l, 0, 0)),
            pl.BlockSpec((None, Q_LORA, MLA_HEADS * LANES), lambda i: (l, 0, 0)),
            pl.BlockSpec((None, 1, LANES), lambda i: (l, 0, 0)),
            tab, tab, tab,
        ],
        out_specs=pl.BlockSpec((1, MLA_HEADS, tm, LANES), lambda i: (i // tpb, 0, i % tpb, 0)),
        compiler_params=_cparams(("parallel",)),
        name="mla_q",
    )(u, gl_all, w_all, gn_all, *rope)


def _kv_kernel(ck_ref, gl_ref, wk_ref, wv_ref, gn_ref, c_ref, s1_ref, s2_ref, k_ref, v_ref):
    blk = ck_ref[...].astype(F32)
    y = _rms(blk[:, :KV_LORA], gl_ref[...], KV_LORA).astype(BF16)
    kn = _dot(y, wk_ref[...])
    v = _dot(y, wv_ref[...])
    kr = blk[:, KV_LORA:KV_LORA + LANES]
    gn = gn_ref[...]
    c, s1, s2 = c_ref[...], s1_ref[...], s2_ref[...]
    for h in range(MLA_HEADS):
        kh = _head_norm_rope(kn[:, h * LANES:(h + 1) * LANES] + kr, gn, c, s1, s2)
        k_ref[0, h] = kh.astype(BF16)
        v_ref[0, h] = v[:, h * LANES:(h + 1) * LANES].astype(BF16)


def _kv_proj(u, gl_all, wk_all, wv_all, gn_all, rope, l, b, n):
    m = u.shape[0]
    tm = min(512, n)
    tpb = n // tm
    tab = pl.BlockSpec((tm, LANES), lambda i: (i % tpb, 0))
    hw = MLA_HEADS * LANES
    o_spec = pl.BlockSpec((1, MLA_HEADS, tm, LANES), lambda i: (i // tpb, 0, i % tpb, 0))
    o_shape = jax.ShapeDtypeStruct((b, MLA_HEADS, n, LANES), BF16)
    return pl.pallas_call(
        _kv_kernel,
        out_shape=(o_shape, o_shape),
        grid=(m // tm,),
        in_specs=[
            pl.BlockSpec((tm, COL_BLK), lambda i: (i, 1)),
            pl.BlockSpec((None, 1, KV_LORA), lambda i: (l, 0, 0)),
            pl.BlockSpec((None, KV_LORA, hw), lambda i: (l, 0, 0)),
            pl.BlockSpec((None, KV_LORA, hw), lambda i: (l, 0, 0)),
            pl.BlockSpec((None, 1, LANES), lambda i: (l, 0, 0)),
            tab, tab, tab,
        ],
        out_specs=(o_spec, o_spec),
        compiler_params=_cparams(("parallel",)),
        name="mla_kv",
    )(u, gl_all, wk_all, wv_all, gn_all, *rope)


def _attn_kernel(*refs, n_src, key_chunk, pv_chunk, q_rows):
    q_ref = refs[0]
    kv = refs[1:1 + 2 * n_src]
    o_ref = refs[1 + 2 * n_src]
    scr = refs[2 + 2 * n_src:]
    chunks = []
    col = 0
    for si in range(n_src):
        nk = kv[2 * si].shape[2]
        for st in range(0, nk, key_chunk):
            sz = min(key_chunk, nk - st)
            chunks.append((si, st, sz, col))
            col += sz
    units = [(hh, r0) for r0 in range(0, q_ref.shape[2], q_rows) for hh in range(2)]

    def scores(u):
        hh, r0 = units[u]
        q = q_ref[0, hh, r0:r0 + q_rows, :]
        mx = None
        for si, st, sz, co in chunks:
            s = _dot_nt(q, kv[2 * si][0, hh, st:st + sz, :])
            scr[u % len(scr)][:, co:co + sz] = s
            cm = jnp.max(s, axis=-1, keepdims=True)
            mx = cm if mx is None else jnp.maximum(mx, cm)
        return mx

    def weighted(u, mx):
        hh, _ = units[u]
        den = None
        acc = None
        for si, st0, sz0, co0 in chunks:
            for off in range(0, sz0, pv_chunk):
                st, co, sz = st0 + off, co0 + off, min(pv_chunk, sz0 - off)
                p = jnp.exp2(scr[u % len(scr)][:, co:co + sz] - mx)
                ps = jnp.sum(p, axis=-1, keepdims=True)
                den = ps if den is None else den + ps
                pv = _dot(p.astype(BF16), kv[2 * si + 1][0, hh, st:st + sz, :])
                acc = pv if acc is None else acc + pv
        return acc / den

    outs = []
    mx_prev = scores(0)
    for u in range(1, len(units)):
        mx = scores(u)
        outs.append(weighted(u - 1, mx_prev))
        mx_prev = mx
    outs.append(weighted(len(units) - 1, mx_prev))
    for i in range(len(units) // 2):
        o_ref[0, i * q_rows:(i + 1) * q_rows, :] = (outs[2 * i] + outs[2 * i + 1]).astype(BF16)


def _attention(q, kvs):
    b, _, nq, _ = q.shape
    tq = min(512, nq)
    q_rows = min(256, tq)
    n_src = len(kvs) // 2
    nk_tot = sum(kvs[2 * i].shape[2] for i in range(n_src))
    n_scr = min(3, 2 * tq // q_rows)
    in_specs = [pl.BlockSpec((1, 2, tq, LANES), lambda bb, hp, i: (bb, hp, i, 0))]
    for a in kvs:
        in_specs.append(pl.BlockSpec((1, 2, a.shape[2], LANES), lambda bb, hp, i: (bb, hp, 0, 0)))
    return pl.pallas_call(
        functools.partial(_attn_kernel, n_src=n_src, key_chunk=1024, pv_chunk=256, q_rows=q_rows),
        out_shape=jax.ShapeDtypeStruct((b, nq, BRANCH_W), BF16),
        grid=(b, MLA_HEADS // 2, nq // tq),
        in_specs=in_specs,
        out_specs=pl.BlockSpec((1, tq, LANES), lambda bb, hp, i: (bb, i, hp)),
        scratch_shapes=[pltpu.VMEM((q_rows, nk_tot), F32) for _ in range(n_scr)],
        compiler_params=_cparams(("parallel", "parallel", "parallel")),
        name="mla_attention",
    )(q, *kvs)


def _local_kernel(a_ref, g_ref, sb_ref, scc_ref, sx_ref,
                  ap_ref, gp_ref, cp_ref, xp_ref, an_ref, gn_ref, cn_ref, xn_ref,
                  cw_ref, lg_ref, lb_ref, sw_ref, y_ref, z_ref, g_scr, h_scr, *, tpb, rows):
    i = pl.program_id(0)
    tm = a_ref.shape[0]
    has_prev = (i % tpb) > 0
    has_next = (i % tpb) < tpb - 1
    zero = jnp.zeros((HALO, COL_BLK), F32)
    f = lambda ref: ref[...].astype(F32)

    def fill(scr, slot, sh, prev, curr, nxt):
        scr[slot, 0:HALO - sh, :] = prev[sh:, :]
        scr[slot, HALO - sh:HALO - sh + tm, :] = curr
        scr[slot, HALO - sh + tm:2 * HALO - sh + tm, :] = nxt

    gp = jnp.where(has_prev, f(ap_ref) * jax.nn.sigmoid(f(gp_ref)), zero)
    gc = f(a_ref) * jax.nn.sigmoid(f(g_ref))
    gn = jnp.where(has_next, f(an_ref) * jax.nn.sigmoid(f(gn_ref)), zero)
    for sh in range(SUBLANES):
        fill(g_scr, sh, sh, gp, gc, gn)
    sc_offs = [HALO - SC_K // 2 + j for j in range(SC_K)]
    hp = jnp.where(has_prev, f(cp_ref) * f(xp_ref), zero)
    hc = f(scc_ref) * f(sx_ref)
    hn = jnp.where(has_next, f(cn_ref) * f(xn_ref), zero)
    for j, o in enumerate(sc_offs):
        fill(h_scr, j, o % SUBLANES, hp, hc, hn)
    lg, lb = lg_ref[...], lb_ref[...]
    for c in range(tm // rows):
        r0 = c * rows
        acc = None
        for j in range(CONF_K):
            o = HALO - CONF_K // 2 + j
            st = r0 + o - o % SUBLANES
            t = cw_ref[j:j + 1, :] * g_scr[o % SUBLANES, st:st + rows, :]
            acc = t if acc is None else acc + t
        mu = jnp.mean(acc, axis=-1, keepdims=True)
        dv = acc - mu
        var = jnp.mean(dv * dv, axis=-1, keepdims=True)
        y = dv * lax.rsqrt(var + EPS) * lg + lb
        y_ref[r0:r0 + rows, :] = _silu(y).astype(BF16)
        acc = None
        for j, o in enumerate(sc_offs):
            st = r0 + o - o % SUBLANES
            t = sw_ref[j:j + 1, :] * h_scr[j, st:st + rows, :]
            acc = t if acc is None else acc + t
        z_ref[r0:r0 + rows, :] = (sb_ref[r0:r0 + rows, :].astype(F32) * acc).astype(BF16)


def _local_branches(u, cw_all, lg_all, lb_all, sw_all, l, n):
    m = u.shape[0]
    tm = min(512, n)
    tpb = n // tm
    hb = tm // HALO
    last = m // HALO - 1

    def cur(cb):
        return pl.BlockSpec((tm, COL_BLK), lambda i: (i, cb))

    def prev(cb):
        return pl.BlockSpec((HALO, COL_BLK), lambda i: (jnp.maximum(i * hb - 1, 0), cb))

    def nxt(cb):
        return pl.BlockSpec((HALO, COL_BLK), lambda i: (jnp.minimum((i + 1) * hb, last), cb))

    o_shape = jax.ShapeDtypeStruct((m, COL_BLK), BF16)
    o_spec = pl.BlockSpec((tm, COL_BLK), lambda i: (i, 0))
    return pl.pallas_call(
        functools.partial(_local_kernel, tpb=tpb, rows=32),
        out_shape=(o_shape, o_shape),
        grid=(m // tm,),
        in_specs=[
            cur(2), cur(3), cur(4), cur(5), cur(6),
            prev(2), prev(3), prev(5), prev(6),
            nxt(2), nxt(3), nxt(5), nxt(6),
            pl.BlockSpec((None, CONF_K, COL_BLK), lambda i: (l, 0, 0)),
            pl.BlockSpec((None, 1, COL_BLK), lambda i: (l, 0, 0)),
            pl.BlockSpec((None, 1, COL_BLK), lambda i: (l, 0, 0)),
            pl.BlockSpec((None, SC_K, COL_BLK), lambda i: (l, 0, 0)),
        ],
        out_specs=(o_spec, o_spec),
        scratch_shapes=[pltpu.VMEM((SUBLANES, tm + 2 * HALO, COL_BLK), F32),
                        pltpu.VMEM((SC_K, tm + 2 * HALO, COL_BLK), F32)],
        compiler_params=_cparams(("parallel",)),
        name="local_convs",
    )(u, u, u, u, u, u, u, u, u, u, u, u, u, cw_all, lg_all, lb_all, sw_all)


def _chan_dft_kernel(f_ref, w_ref, o_ref):
    r = _dot(f_ref[...], w_ref[...])
    o_ref[0, 0] = r[:, :COL_BLK].astype(BF16)
    o_ref[0, 1] = r[:, COL_BLK:].astype(BF16)


def _chan_dft(u, w_chan, b, n):
    m = u.shape[0]
    tm = min(512, n)
    tpb = n // tm
    return pl.pallas_call(
        _chan_dft_kernel,
        out_shape=jax.ShapeDtypeStruct((b, 2, n, COL_BLK), BF16),
        grid=(m // tm,),
        in_specs=[
            pl.BlockSpec((tm, COL_BLK), lambda i: (i, 7)),
            pl.BlockSpec((COL_BLK, 2 * COL_BLK), lambda i: (0, 0)),
        ],
        out_specs=pl.BlockSpec((1, 2, tm, COL_BLK), lambda i: (i // tpb, 0, i % tpb, 0)),
        compiler_params=_cparams(("parallel",)),
        name="fnet_chan_dft",
    )(u, w_chan)


def _pos_dft_kernel(t_ref, ab_ref, o_ref, acc_ref):
    k = pl.program_id(2)

    @pl.when(k == 0)
    def _():
        acc_ref[...] = jnp.zeros_like(acc_ref)

    acc_ref[...] += _dot(t_ref[...], ab_ref[0])

    @pl.when(k == pl.num_programs(2) - 1)
    def _():
        o_ref[0] = acc_ref[...].astype(BF16)


def _pos_dft(t_pos, ab):
    b, n2, _ = ab.shape
    n = n2 // 2
    tm = min(1024, n)
    tk = min(2048, n2)
    return pl.pallas_call(
        _pos_dft_kernel,
        out_shape=jax.ShapeDtypeStruct((b, n, COL_BLK), BF16),
        grid=(b, n // tm, n2 // tk),
        in_specs=[
            pl.BlockSpec((tm, tk), lambda bb, i, k: (i, k)),
            pl.BlockSpec((1, tk, COL_BLK), lambda bb, i, k: (bb, k, 0)),
        ],
        out_specs=pl.BlockSpec((1, tm, COL_BLK), lambda bb, i, k: (bb, i, 0)),
        scratch_shapes=[pltpu.VMEM((tm, COL_BLK), F32)],
        compiler_params=_cparams(("parallel", "parallel", "arbitrary")),
        name="fnet_pos_dft",
    )(t_pos, ab)


def _fft_stage1_kernel(g_ref, w_ref, tc_ref, ts_ref, o_ref):
    r = tc_ref.shape[0]
    g = g_ref[0]
    y = _dot(w_ref[...], g.reshape(2 * r, g.shape[-1]))
    yr, yi = y[:r], y[r:]
    tc, ts = tc_ref[...], ts_ref[...]
    o_ref[0, 0] = (yr * tc + yi * ts).astype(BF16)
    o_ref[0, 1] = (yi * tc - yr * ts).astype(BF16)


def _fft_stage2_kernel(y_ref, w_ref, o_ref):
    w = w_ref[...]
    for kk in range(y_ref.shape[2]):
        op = jnp.concatenate([y_ref[0, 0, kk], y_ref[0, 1, kk]], axis=0)
        o_ref[0, kk] = _dot(w, op).astype(BF16)


def _pos_fft(ab4, tabs):
    w1, w2, tc, ts = tabs
    b, _, n, _ = ab4.shape
    r = w2.shape[0]
    lanes = r * COL_BLK
    tn = 8 * COL_BLK
    y = pl.pallas_call(
        _fft_stage1_kernel,
        out_shape=jax.ShapeDtypeStruct((b, 2, r, lanes), BF16),
        grid=(b, lanes // tn),
        in_specs=[
            pl.BlockSpec((1, 2, r, tn), lambda bb, j: (bb, 0, 0, j)),
            pl.BlockSpec((2 * r, 2 * r), lambda bb, j: (0, 0)),
            pl.BlockSpec((r, tn), lambda bb, j: (0, j)),
            pl.BlockSpec((r, tn), lambda bb, j: (0, j)),
        ],
        out_specs=pl.BlockSpec((1, 2, r, tn), lambda bb, j: (bb, 0, 0, j)),
        compiler_params=_cparams(("parallel", "parallel")),
        name="fnet_fft_stage1",
    )(ab4.reshape(b, 2, r, lanes), w1, tc, ts)
    kb = 8
    out = pl.pallas_call(
        _fft_stage2_kernel,
        out_shape=jax.ShapeDtypeStruct((b, r, r, COL_BLK), BF16),
        grid=(b, r // kb),
        in_specs=[
            pl.BlockSpec((1, 2, kb, r, COL_BLK), lambda bb, j: (bb, 0, j, 0, 0)),
            pl.BlockSpec((r, 2 * r), lambda bb, j: (0, 0)),
        ],
        out_specs=pl.BlockSpec((1, kb, r, COL_BLK), lambda bb, j: (bb, j, 0, 0)),
        compiler_params=_cparams(("parallel", "parallel")),
        name="fnet_fft_stage2",
    )(y.reshape(b, 2, r, r, COL_BLK), w2)
    return jnp.swapaxes(out, 1, 2).reshape(b * n, COL_BLK)


def _merge_kernel(hx_ref, y0_ref, y1_ref, y2_ref, y3_ref, wg0_ref, wg1_ref, wg2_ref, wg3_ref, wb_ref, o_ref):
    hx = hx_ref[...]
    ys = (y0_ref, y1_ref, y2_ref, y3_ref)
    wgs = (wg0_ref, wg1_ref, wg2_ref, wg3_ref)
    acc = None
    for k in range(N_BRANCH):
        t = jax.nn.sigmoid(_dot(hx, wgs[k][...])) * _dot(ys[k][...], wb_ref[k])
        acc = t if acc is None else acc + t
    o_ref[...] = acc.astype(BF16)


def _merge(hx, ys, wg_all, wb_all, l):
    m = hx.shape[0]
    tm = min(1024, m)
    tn = 512
    nj = D_MODEL // tn
    y_spec = pl.BlockSpec((tm, BRANCH_W), lambda i, j: (i, 0))
    wg_specs = [pl.BlockSpec((None, D_MODEL, tn), lambda i, j, k=k: (l, 0, k * nj + j)) for k in range(N_BRANCH)]
    return pl.pallas_call(
        _merge_kernel,
        out_shape=jax.ShapeDtypeStruct((m, D_MODEL), BF16),
        grid=(m // tm, nj),
        in_specs=[pl.BlockSpec((tm, D_MODEL), lambda i, j: (i, 0)), y_spec, y_spec, y_spec, y_spec,
                  *wg_specs,
                  pl.BlockSpec((None, N_BRANCH, BRANCH_W, tn), lambda i, j: (l, 0, 0, j))],
        out_specs=pl.BlockSpec((tm, tn), lambda i, j: (i, j)),
        compiler_params=_cparams(("parallel", "parallel")),
        name="branch_merge",
    )(hx, *ys, wg_all, wg_all, wg_all, wg_all, wb_all)


def _out_proj_kernel(m_ref, w_ref, x_ref, g_ref, o_ref):
    o_ref[...] = x_ref[...] + g_ref[0] * _dot(m_ref[...], w_ref[...])


def _out_proj(mm, w_all, x2, mod3, l, n):
    m = mm.shape[0]
    tm = min(1024, n)
    tpb = n // tm
    tn = 512
    nj = D_MODEL // tn
    return pl.pallas_call(
        _out_proj_kernel,
        out_shape=jax.ShapeDtypeStruct((m, D_MODEL), F32),
        grid=(m // tm, nj),
        in_specs=[
            pl.BlockSpec((tm, D_MODEL), lambda i, j: (i, 0)),
            pl.BlockSpec((None, D_MODEL, tn), lambda i, j: (l, 0, j)),
            pl.BlockSpec((tm, tn), lambda i, j: (i, j)),
            pl.BlockSpec((1, 1, tn), lambda i, j: (i // tpb, 0, 2 * nj + j)),
        ],
        out_specs=pl.BlockSpec((tm, tn), lambda i, j: (i, j)),
        compiler_params=_cparams(("parallel", "parallel")),
        name="out_proj_residual",
    )(mm, w_all, x2, mod3)


def _ffn_norm(x, g, sh, sc):
    return _rms(x, g, D_MODEL) * (1.0 + sc) + sh


def _router_kernel(x_ref, g_ref, sh_ref, sc_ref, wr_ref, aff_ref):
    hx = _ffn_norm(x_ref[...], g_ref[...], sh_ref[0], sc_ref[0])
    lt = _dot3(wr_ref[...], hx, nt=True)
    e = jnp.exp(lt - jnp.max(lt, axis=0, keepdims=True))
    aff_ref[0] = e / jnp.sum(e, axis=0, keepdims=True)


def _router(x2, g_all, mod3, wr_t_all, l, b, n):
    m = x2.shape[0]
    tm = min(512, n)
    tpb = n // tm
    return pl.pallas_call(
        _router_kernel,
        out_shape=jax.ShapeDtypeStruct((b, N_EXPERTS, n), F32),
        grid=(m // tm,),
        in_specs=[
            pl.BlockSpec((tm, D_MODEL), lambda i: (i, 0)),
            pl.BlockSpec((None, 1, D_MODEL), lambda i: (l, 0, 0)),
            pl.BlockSpec((1, 1, D_MODEL), lambda i: (i // tpb, 0, 3)),
            pl.BlockSpec((1, 1, D_MODEL), lambda i: (i // tpb, 0, 4)),
            pl.BlockSpec((None, N_EXPERTS, D_MODEL), lambda i: (l, 0, 0)),
        ],
        out_specs=pl.BlockSpec((1, N_EXPERTS, tm), lambda i: (i // tpb, 0, i % tpb)),
        compiler_params=_cparams(("parallel",)),
        name="norm_router",
    )(x2, g_all, mod3, mod3, wr_t_all)


def _prefix_count(mask01, tri):
    n = mask01.shape[1]
    carry = jnp.zeros((mask01.shape[0], 1), F32)
    outs = []
    for c in range(n // LANES):
        loc = _dot(mask01[:, c * LANES:(c + 1) * LANES], tri) + carry
        outs.append(loc)
        carry = loc[:, LANES - 1:LANES]
    return jnp.concatenate(outs, axis=1)


def _topk_kernel(a_ref, idx_ref, gate_ref, *, cap):
    a = a_ref[0]
    ne, n = a.shape
    bits = pltpu.bitcast(a, jnp.int32)

    def search(i, t):
        cand = t | (jnp.int32(1) << (30 - i))
        c = jnp.sum(jnp.where(bits >= cand, 1.0, 0.0), axis=-1, keepdims=True)
        return jnp.where(c >= cap, cand, t)

    thr = lax.fori_loop(0, 31, search, jnp.zeros((ne, 1), jnp.int32))
    gt = bits > thr
    eq = bits == thr
    need = cap - jnp.sum(jnp.where(gt, 1.0, 0.0), axis=-1, keepdims=True)
    row = lax.broadcasted_iota(jnp.int32, (LANES, LANES), 0)
    colm = lax.broadcasted_iota(jnp.int32, (LANES, LANES), 1)
    tri = jnp.where(row <= colm, 1.0, 0.0).astype(BF16)
    eq_rank = _prefix_count(jnp.where(eq, 1.0, 0.0).astype(BF16), tri)
    sel = gt | (eq & (eq_rank <= need))
    cnt = _prefix_count(jnp.where(sel, 1.0, 0.0).astype(BF16), tri)
    slot = jnp.where(sel, cnt - 1.0, -1.0)

    tok = lax.broadcasted_iota(jnp.int32, (1, n), 1)
    t_hi = (tok >> 6).astype(F32)
    t_lo = (tok & 63).astype(F32)
    r_iota = lax.broadcasted_iota(jnp.int32, (cap, 1), 0).astype(F32)
    zeros3 = jnp.zeros((3, n), F32)
    for e in range(ne):
        onehot = jnp.where(slot[e:e + 1, :] == r_iota, 1.0, 0.0).astype(BF16)
        ae = a[e:e + 1, :]
        a1 = ae.astype(BF16).astype(F32)
        a2 = (ae - a1).astype(BF16).astype(F32)
        a3 = ae - a1 - a2
        rows = jnp.concatenate([t_hi, t_lo, a1, a2, a3, zeros3], axis=0).astype(BF16)
        res = _dot_nt(rows, onehot)
        idx_ref[0, e:e + 1, :] = (res[0:1] * 64.0 + res[1:2]).astype(jnp.int32)
        gate_ref[0, e:e + 1, :] = res[2:3] + res[3:4] + res[4:5]


def _topk(aff_t, cap):
    b, ne, n = aff_t.shape
    return pl.pallas_call(
        functools.partial(_topk_kernel, cap=cap),
        out_shape=(jax.ShapeDtypeStruct((b, ne, cap), jnp.int32),
                   jax.ShapeDtypeStruct((b, ne, cap), F32)),
        grid=(b,),
        in_specs=[pl.BlockSpec((1, ne, n), lambda bb: (bb, 0, 0))],
        out_specs=(pl.BlockSpec((1, ne, cap), lambda bb: (bb, 0, 0)),
                   pl.BlockSpec((1, ne, cap), lambda bb: (bb, 0, 0))),
        compiler_params=_cparams(("parallel",)),
        name="expert_topk",
    )(aff_t)


def _swiglu_rows(xh, w1_ref, w3_ref, w2_ref, n_chunks):
    fc = EXPERT_FF // n_chunks
    y = None
    for c in range(n_chunks):
        cs = slice(c * fc, (c + 1) * fc)
        h = (_silu(_dot(xh, w1_ref[:, cs])) * _dot(xh, w3_ref[:, cs])).astype(BF16)
        t = _dot(h, w2_ref[cs, :])
        y = t if y is None else y + t
    return y


def _expert_seq_kernel(idx_ref, xin_hbm, gf_ref, sh_ref, sc_ref, w1_ref, w3_ref, w2_ref, gate_ref, g2_ref,
                       xo_hbm, xb, xh, sem_g, sem_w, *, n, cap, bg):
    del xin_hbm
    e = pl.program_id(0)
    g = pl.program_id(1)
    rows = bg * cap

    def for_rows(fn):
        for bi in range(bg):
            b_ = g * bg + bi
            base = (b_ * N_EXPERTS + e) * cap

            def body(r, carry, b_=b_, base=base, bi=bi):
                fn(b_ * n + idx_ref[base + r], bi * cap + r)
                return carry

            lax.fori_loop(0, cap, body, 0)

    for_rows(lambda src, dst: pltpu.make_async_copy(
        xo_hbm.at[pl.ds(src, 1)], xb.at[pl.ds(dst, 1)], sem_g).start())
    pltpu.make_async_copy(xo_hbm.at[pl.ds(0, rows)], xb, sem_g).wait()
    gf = gf_ref[...]
    for bi in range(bg):
        sl = slice(bi * cap, (bi + 1) * cap)
        xh[sl, :] = _ffn_norm(xb[sl, :], gf, sh_ref[bi], sc_ref[bi]).astype(BF16)
    y = _swiglu_rows(xh[...], w1_ref, w3_ref, w2_ref, 1)
    for bi in range(bg):
        sl = slice(bi * cap, (bi + 1) * cap)
        xb[sl, :] = xb[sl, :] + y[sl, :] * gate_ref[bi, 0] * g2_ref[bi]
    for_rows(lambda src, dst: pltpu.make_async_copy(
        xb.at[pl.ds(dst, 1)], xo_hbm.at[pl.ds(src, 1)], sem_w).start())
    pltpu.make_async_copy(xb, xo_hbm.at[pl.ds(0, rows)], sem_w).wait()


def _expert_pipe_kernel(idx_ref, xin_hbm, gf_ref, sh_ref, sc_ref, w1_ref, w3_ref, w2_ref, gate_ref, g2_ref,
                        xo_hbm, xb, xh, sem_g, sem_w, *, n, cap, nbg, n_chunks):
    del xin_hbm
    e = pl.program_id(0)
    g = pl.program_id(1)
    s = e * nbg + g
    total = N_EXPERTS * nbg
    slot = s % 3
    nxt_slot = (s + 1) % 3
    prv_slot = (s + 2) % 3

    def rows_of(t):
        e_, g_ = t // nbg, t % nbg
        return (g_ * N_EXPERTS + e_) * cap, g_ * n

    def gather(base, row0, slot_, r):
        return pltpu.make_async_copy(xo_hbm.at[pl.ds(row0 + idx_ref[base + r], 1)],
                                     xb.at[slot_, pl.ds(r, 1)], sem_g.at[slot_])

    def scatter(base, row0, slot_, r):
        return pltpu.make_async_copy(xb.at[slot_, pl.ds(r, 1)],
                                     xo_hbm.at[pl.ds(row0 + idx_ref[base + r], 1)], sem_w)

    def loop_rows(fn):
        def body(r, carry):
            fn(r)
            return carry
        lax.fori_loop(0, cap, body, 0)

    def wait_gather(slot_):
        pltpu.make_async_copy(xo_hbm.at[pl.ds(0, cap)], xb.at[slot_], sem_g.at[slot_]).wait()

    def wait_scatter():
        pltpu.make_async_copy(xb.at[0], xo_hbm.at[pl.ds(0, cap)], sem_w).wait()

    cur = rows_of(s)
    prv = rows_of(jnp.where(s == 0, nbg - 1, s - 1))
    nxt = rows_of(jnp.minimum(s + 1, total - 1))

    @pl.when(s == 0)
    def _():
        loop_rows(lambda r: gather(*cur, slot, r).start())
        loop_rows(lambda r: gather(*prv, prv_slot, r).start())
        wait_gather(prv_slot)

    @pl.when(s > 0)
    def _():
        wait_scatter()

    for r in range(cap):
        gather(*nxt, nxt_slot, r).start()
        scatter(*prv, prv_slot, r).start()
    wait_gather(slot)

    xh[...] = _ffn_norm(xb[slot], gf_ref[...], sh_ref[0], sc_ref[0]).astype(BF16)
    y = _swiglu_rows(xh[...], w1_ref, w3_ref, w2_ref, n_chunks)
    xb[slot] = xb[slot] + y * gate_ref[0, 0] * g2_ref[0]

    @pl.when(s == total - 1)
    def _():
        wait_gather(nxt_slot)
        wait_scatter()
        loop_rows(lambda r: scatter(*cur, slot, r).start())
        wait_scatter()


def _experts(idx_flat, x2, gf_all, w1_all, w3_all, w2_all, gate_col, mod3, l, b, n, cap, bg):
    nbg = b // bg
    rows = bg * cap
    pipelined = bg == 1 and nbg >= 3
    if pipelined:
        body = functools.partial(_expert_pipe_kernel, n=n, cap=cap, nbg=nbg, n_chunks=4)
        scratch = [pltpu.VMEM((3, rows, D_MODEL), F32), pltpu.VMEM((rows, D_MODEL), BF16),
                   pltpu.SemaphoreType.DMA((3,)), pltpu.SemaphoreType.DMA]
    else:
        body = functools.partial(_expert_seq_kernel, n=n, cap=cap, bg=bg)
        scratch = [pltpu.VMEM((rows, D_MODEL), F32), pltpu.VMEM((rows, D_MODEL), BF16),
                   pltpu.SemaphoreType.DMA, pltpu.SemaphoreType.DMA]
    return pl.pallas_call(
        body,
        out_shape=jax.ShapeDtypeStruct(x2.shape, F32),
        grid_spec=pltpu.PrefetchScalarGridSpec(
            num_scalar_prefetch=1,
            grid=(N_EXPERTS, nbg),
            in_specs=[
                pl.BlockSpec(memory_space=pl.ANY),
                pl.BlockSpec((None, 1, D_MODEL), lambda e, g, idx: (l, 0, 0)),
                pl.BlockSpec((bg, 1, D_MODEL), lambda e, g, idx: (g, 0, 3)),
                pl.BlockSpec((bg, 1, D_MODEL), lambda e, g, idx: (g, 0, 4)),
                pl.BlockSpec((None, None, D_MODEL, EXPERT_FF), lambda e, g, idx: (l, e, 0, 0)),
                pl.BlockSpec((None, None, D_MODEL, EXPERT_FF), lambda e, g, idx: (l, e, 0, 0)),
                pl.BlockSpec((None, None, EXPERT_FF, D_MODEL), lambda e, g, idx: (l, e, 0, 0)),
                pl.BlockSpec((bg, 1, cap, 1), lambda e, g, idx: (g, e, 0, 0)),
                pl.BlockSpec((bg, 1, D_MODEL), lambda e, g, idx: (g, 0, 5)),
            ],
            out_specs=pl.BlockSpec(memory_space=pl.ANY),
            scratch_shapes=scratch,
        ),
        input_output_aliases={1: 0},
        compiler_params=_cparams(("arbitrary", "arbitrary")),
        name="expert_ffn",
    )(idx_flat, x2, gf_all, mod3, mod3, w1_all, w3_all, w2_all, gate_col, mod3)


def _rope_tables(n, use_rope):
    ones_nope = jnp.ones((n, QK_NOPE), F32)
    pad = jnp.zeros((n, LANES - QK_HEAD), F32)
    if not use_rope:
        z = jnp.zeros((n, LANES), F32)
        return jnp.concatenate([ones_nope, jnp.ones((n, QK_ROPE), F32), pad], axis=1), z, z
    rows = n // GRID_W
    t_row = jnp.repeat(jnp.arange(rows, dtype=F32), GRID_W)
    t_col = jnp.tile(jnp.arange(GRID_W, dtype=F32), rows)
    inv = ROPE_THETA ** (-jnp.arange(ROPE_PAIRS, dtype=F32) / ROPE_PAIRS)
    ang = jnp.stack([t_row[:, None] * inv, t_col[:, None] * inv], axis=1)
    cos, sin = jnp.cos(ang), jnp.sin(ang)
    z8 = jnp.zeros_like(sin)
    c = jnp.stack([cos, cos], axis=2).reshape(n, QK_ROPE)
    s1 = jnp.stack([-sin, z8], axis=2).reshape(n, QK_ROPE)
    s2 = jnp.stack([z8, sin], axis=2).reshape(n, QK_ROPE)
    zn = jnp.zeros((n, QK_NOPE), F32)
    return (jnp.concatenate([ones_nope, c, pad], axis=1),
            jnp.concatenate([zn, s1, pad], axis=1),
            jnp.concatenate([zn, s2, pad], axis=1))


def _dft_tables(size, split):
    j = jnp.arange(size, dtype=jnp.int32)[:, None]
    na = size // split
    a = jnp.arange(na, dtype=jnp.int32)[None, :]
    bb = jnp.arange(split, dtype=jnp.int32)[None, :]
    ang_a = (2.0 * math.pi / na) * ((j * a) % na).astype(F32)
    ang_b = (2.0 * math.pi / size) * ((j * bb) % size).astype(F32)
    ca, sa, cb, sb = jnp.cos(ang_a), jnp.sin(ang_a), jnp.cos(ang_b), jnp.sin(ang_b)
    cos = (ca[:, :, None] * cb[:, None, :] - sa[:, :, None] * sb[:, None, :]).reshape(size, size)
    sin = (sa[:, :, None] * cb[:, None, :] + ca[:, :, None] * sb[:, None, :]).reshape(size, size)
    return cos, sin


def _pos_dft_table(n):
    split = 1
    while split * split < n:
        split *= 2
    cos, sin = _dft_tables(n, split)
    return (jnp.concatenate([cos, sin], axis=1) * (n ** -0.5)).astype(BF16)


def _pos_fft_tables(n):
    r = math.isqrt(n)
    if n < 1024 or r * r != n:
        return None
    cos, sin = _dft_tables(r, 8)
    scale = r ** -0.5
    w1 = (jnp.block([[cos, sin], [-sin, cos]]) * scale).astype(BF16)
    w2 = (jnp.concatenate([cos, sin], axis=1) * scale).astype(BF16)
    k1 = jnp.arange(r, dtype=jnp.int32)[:, None]
    n2 = jnp.arange(r, dtype=jnp.int32)[None, :]
    ang = (2.0 * math.pi / n) * ((k1 * n2) % n).astype(F32)
    expand = lambda t: jnp.broadcast_to(t[:, :, None], (r, r, COL_BLK)).reshape(r, r * COL_BLK)
    return w1, w2, expand(jnp.cos(ang)), expand(jnp.sin(ang))


def _chan_dft_table():
    cos, sin = _dft_tables(FNET_GW, 16)
    eye = jnp.eye(FNET_GROUPS, dtype=F32)
    scale = FNET_GW ** -0.5
    return jnp.concatenate([jnp.kron(eye, cos * scale), jnp.kron(eye, -sin * scale)], axis=1).astype(BF16)


def _pad_heads(w, width):
    nl, k, _ = w.shape
    w = w.reshape(nl, k, MLA_HEADS, width)
    return jnp.pad(w, ((0, 0), (0, 0), (0, 0), (0, LANES - width))).reshape(nl, k, MLA_HEADS * LANES)


def _prep_weights(w_in, w_uq, w_uk, w_uv, g_qn, g_kn, w_gate, w_branch, w_out, w_router, w_e1, w_e3, w_e2):
    nl = w_in.shape[0]
    z = lambda w: jnp.zeros((nl, D_MODEL, w), F32)
    w_in_p = jnp.concatenate([
        w_in[:, :, :OFF_KR], z(QK_NOPE), w_in[:, :, OFF_KR:OFF_CONF], z(2 * COL_BLK - OFF_KR - QK_HEAD),
        w_in[:, :, OFF_CONF:]], axis=2).astype(BF16)
    w_uv_h = w_uv.reshape(nl, KV_LORA, MLA_HEADS // 2, 2, V_HEAD)
    zv = jnp.zeros_like(w_uv_h[:, :, :, 0])
    w_uv_p = jnp.stack([jnp.concatenate([w_uv_h[:, :, :, 0], zv], axis=-1),
                        jnp.concatenate([zv, w_uv_h[:, :, :, 1]], axis=-1)], axis=3)
    padg = lambda g: jnp.pad(g, ((0, 0), (0, LANES - QK_HEAD))).reshape(nl, 1, LANES)
    return dict(
        w_in=w_in_p,
        w_uq=_pad_heads(w_uq, QK_HEAD).astype(BF16),
        w_uk=_pad_heads(w_uk, QK_NOPE).astype(BF16),
        w_uv=w_uv_p.reshape(nl, KV_LORA, MLA_HEADS * LANES).astype(BF16),
        g_qn=padg(g_qn), g_kn=padg(g_kn),
        w_gate=w_gate.astype(BF16), w_branch=w_branch.astype(BF16), w_out=w_out.astype(BF16),
        w_router_t=jnp.swapaxes(w_router, 1, 2),
        w_e1=w_e1.astype(BF16), w_e3=w_e3.astype(BF16), w_e2=w_e2.astype(BF16),
    )


def _mix_inputs(x2, mod3, p, g_mix, l, n):
    return _in_proj(x2, g_mix, mod3, p["w_in"], l, n)


def _keys_values(u, p, g_kv_lora, rope, l, b, n):
    return _kv_proj(u, g_kv_lora, p["w_uk"], p["w_uv"], p["g_kn"], rope, l, b, n)


def _mix_finish(x2, u, hx, att, mod3, p, loc, t_pos, w_chan, l, b, n):
    m = x2.shape[0]
    y_conf, y_sc = _local_branches(u, loc["conf_dw"], loc["conf_ln_g"], loc["conf_ln_b"], loc["sc_dw"], l, n)
    ab = _chan_dft(u, w_chan, b, n)
    if isinstance(t_pos, tuple):
        y_f = _pos_fft(ab, t_pos)
    else:
        y_f = _pos_dft(t_pos, ab.reshape(b, 2 * n, COL_BLK)).reshape(m, COL_BLK)
    mm = _merge(hx, (att.reshape(m, BRANCH_W), y_conf, y_sc, y_f), p["w_gate"], p["w_branch"], l)
    return _out_proj(mm, p["w_out"], x2, mod3, l, n)


def _moe(x2, mod3, p, g_ffn, l, b, n, bg):
    cap = EC_CAPACITY_FACTOR * n // N_EXPERTS
    aff_t = _router(x2, g_ffn, mod3, p["w_router_t"], l, b, n)
    idx, gate = _topk(aff_t, cap)
    return _experts(idx.reshape(-1), x2, g_ffn, p["w_e1"], p["w_e3"], p["w_e2"],
                    gate.reshape(b, N_EXPERTS, cap, 1), mod3, l, b, n, cap, bg)


def kernel(x, c, ctx, c_ctx, w_mod, b_mod, g_mix, g_ffn, w_in, g_q_lora, w_uq, g_kv_lora, w_uk, w_uv,
           g_qn, g_kn, conf_dw, conf_ln_g, conf_ln_b, sc_dw, w_branch, w_gate, w_out, w_router,
           w_e1, w_e3, w_e2):
    b, n, _ = x.shape
    nc = ctx.shape[1]
    nl = w_mod.shape[0]
    p = _prep_weights(w_in, w_uq, w_uk, w_uv, g_qn, g_kn, w_gate, w_branch, w_out, w_router, w_e1, w_e3, w_e2)
    r3 = lambda g: g.reshape(nl, 1, g.shape[-1])
    g_mix3, g_ffn3, g_ql3, g_kvl3 = r3(g_mix), r3(g_ffn), r3(g_q_lora), r3(g_kv_lora)
    loc = dict(conf_dw=conf_dw, conf_ln_g=r3(conf_ln_g), conf_ln_b=r3(conf_ln_b), sc_dw=sc_dw)
    rope_x = _rope_tables(n, True)
    rope_c = _rope_tables(nc, False)
    t_pos_x = _pos_fft_tables(n) or _pos_dft_table(n)
    t_pos_c = _pos_fft_tables(nc) or _pos_dft_table(nc)
    w_chan = _chan_dft_table()

    cond = jnp.concatenate([c, c_ctx[None, :], jnp.zeros((8 - b - 1, D_MODEL), F32)], axis=0)
    mods = _mod_all(cond, w_mod, b_mod)

    x2 = x.reshape(b * n, D_MODEL)
    xc2 = ctx.reshape(b * nc, D_MODEL)
    for l in range(nl):
        last = l == nl - 1
        mod_x = mods[l, :b].reshape(b, 1, 6 * D_MODEL)
        mod_c = jnp.broadcast_to(mods[l, b].reshape(1, 1, 6 * D_MODEL), (b, 1, 6 * D_MODEL))

        u_x, hx = _mix_inputs(x2, mod_x, p, g_mix3, l, n)
        u_c, hc = _mix_inputs(xc2, mod_c, p, g_mix3, l, nc)
        k_c, v_c = _keys_values(u_c, p, g_kvl3, rope_c, l, b, nc)
        k_x, v_x = _keys_values(u_x, p, g_kvl3, rope_x, l, b, n)
        q_x = _q_proj(u_x, g_ql3, p["w_uq"], p["g_qn"], rope_x, l, b, n)
        att_x = _attention(q_x, (k_c, v_c, k_x, v_x))
        x2 = _mix_finish(x2, u_x, hx, att_x, mod_x, p, loc, t_pos_x, w_chan, l, b, n)
        if not last:
            q_c = _q_proj(u_c, g_ql3, p["w_uq"], p["g_qn"], rope_c, l, b, nc)
            att_c = _attention(q_c, (k_c, v_c))
            xc2 = _mix_finish(xc2, u_c, hc, att_c, mod_c, p, loc, t_pos_c, w_chan, l, b, nc)

        x2 = _moe(x2, mod_x, p, g_ffn3, l, b, n, 1 if n >= 1024 else b)
        if not last:
            xc2 = _moe(xc2, mod_c, p, g_ffn3, l, b, nc, b)
    return x2.reshape(b, n, D_MODEL)
```
